```python
import math
import jax, jax.numpy as jnp
from jax import lax
import numpy as np

D_MODEL = 1024
BATCH = 8
SEQ = 4096
DEPTH = 1
DEC_BATCH = 128
DEC_SEQ = 8
PAST_LEN = 8192
PAGE_SIZE = 128

N_HEADS = 8
HEAD_DIM = 64
ATTN_WIDTH = N_HEADS * HEAD_DIM
IDX_HEADS = 8
IDX_DIM = 64
TOPK_MAX = 256
ROPE_THETA = 500000.0
ROPE_FRACTION = 4
SSM_GROUP = 16
SSM_GROUPS = 32
SSM_WIDTH = SSM_GROUP * SSM_GROUPS
SSM_STATE = 64
D_FF = -(-8 * D_MODEL // (3 * 256)) * 256
PLE_DIM = 256
Q_BLOCK = 128
EPS = 1e-6
NEG = -1e30
IN_SIZES = (ATTN_WIDTH, ATTN_WIDTH, ATTN_WIDTH, IDX_HEADS * IDX_DIM, IDX_DIM, IDX_HEADS, SSM_WIDTH, D_MODEL, D_MODEL)
IN_WIDTH = sum(IN_SIZES)

kernel_name = "hybrid_dsa_s5_gated_decode_step"


def rmsnorm(x, g):
    xf = x.astype(jnp.float32)
    y = xf * lax.rsqrt(jnp.mean(xf * xf, axis=-1, keepdims=True) + EPS) * g.astype(jnp.float32)
    return y.astype(x.dtype)


def rope(x, pos):
    d = x.shape[-1]
    rot = d // ROPE_FRACTION
    half = rot // 2
    inv_freq = ROPE_THETA ** (-jnp.arange(0, rot, 2, dtype=jnp.float32) / rot)
    ang = pos.astype(jnp.float32)[:, None] * inv_freq[None, :]
    c = jnp.cos(ang)[:, None, :]
    s = jnp.sin(ang)[:, None, :]
    xf = x.astype(jnp.float32)
    x1, x2, xp = xf[..., :half], xf[..., half:rot], xf[..., rot:]
    out = jnp.concatenate([x1 * c - x2 * s, x2 * c + x1 * s, xp], axis=-1)
    return out.astype(x.dtype)


def mix_inputs(h, w_in, pos):
    B, T = h.shape[:2]
    z = h @ w_in
    splits = np.cumsum(IN_SIZES)[:-1].tolist()
    q, k, v, qi, ki, wi, u, ga, gb = jnp.split(z, splits, axis=-1)
    q = rope(q.reshape(B, T, N_HEADS, HEAD_DIM), pos)
    k = rope(k.reshape(B, T, N_HEADS, HEAD_DIM), pos)
    v = v.reshape(B, T, N_HEADS, HEAD_DIM)
    qi = rope(qi.reshape(B, T, IDX_HEADS, IDX_DIM), pos)
    ki = rope(ki[:, :, None, :], pos)[:, :, 0, :]
    wi = wi * (IDX_HEADS ** -0.5)
    return q, k, v, qi, ki, wi, u, ga, gb


def index_topk(qi, wi, ki_all, qpos, n_sel):
    L = ki_all.shape[1]
    s = jnp.einsum('bthd,bld->bthl', qi.astype(jnp.float32), ki_all.astype(jnp.float32)) * (IDX_DIM ** -0.5)
    score = jnp.einsum('bthl,bth->btl', jax.nn.relu(s), wi.astype(jnp.float32))
    visible = jnp.arange(L)[None, :] <= qpos[:, None]
    score = jnp.where(visible[None], score, NEG)
    _, idx = lax.top_k(score, n_sel)
    valid = idx <= qpos[None, :, None]
    return idx, valid


def gather_rows(rows, idx):
    return jax.vmap(lambda r, i: r[i])(rows, idx)


def sparse_attend(q, k_sel, v_sel, valid):
    B, T = q.shape[:2]
    logits = jnp.einsum('bthd,btkhd->bthk', q.astype(jnp.float32), k_sel.astype(jnp.float32)) * (HEAD_DIM ** -0.5)
    logits = jnp.where(valid[:, :, None, :], logits, NEG)
    p = jax.nn.softmax(logits, axis=-1)
    out = jnp.einsum('bthk,btkhd->bthd', p, v_sel.astype(jnp.float32))
    return out.reshape(B, T, ATTN_WIDTH).astype(q.dtype)


def prompt_attention(q, k, v, qi, ki, wi):
    B, S = q.shape[:2]
    nb = S // Q_BLOCK
    n_sel = min(TOPK_MAX, S // 4)

    def block(args):
        qb, qib, wib, start = args
        qpos = start + jnp.arange(Q_BLOCK)
        idx, valid = index_topk(qib, wib, ki, qpos, n_sel)
        return sparse_attend(qb, gather_rows(k, idx), gather_rows(v, idx), valid)

    def to_blocks(a):
        return a.reshape(B, nb, Q_BLOCK, *a.shape[2:]).swapaxes(0, 1)

    out = lax.map(block, (to_blocks(q), to_blocks(qi), to_blocks(wi), jnp.arange(nb) * Q_BLOCK))
    return out.swapaxes(0, 1).reshape(B, S, ATTN_WIDTH)


def sample_attention(q, k, v, qi, ki, wi, cache_k, cache_v, cache_kidx, page_table):
    DB, T = q.shape[:2]
    past = page_table.shape[1] * PAGE_SIZE
    L = past + T
    n_sel = min(TOPK_MAX, L // 4)
    ki_past = cache_kidx[page_table].reshape(DB, past, IDX_DIM)
    ki_all = jnp.concatenate([ki_past, ki.astype(ki_past.dtype)], axis=1)
    qpos = past + jnp.arange(T)
    idx, valid = index_topk(qi, wi, ki_all, qpos, n_sel)
    in_past = idx < past
    pidx = jnp.minimum(idx, past - 1)
    page = jnp.take_along_axis(page_table, (pidx // PAGE_SIZE).reshape(DB, -1), axis=1).reshape(idx.shape)
    phys = page * PAGE_SIZE + pidx % PAGE_SIZE
    nidx = jnp.clip(idx - past, 0, T - 1)
    flat_k = cache_k.reshape(-1, N_HEADS, HEAD_DIM)
    flat_v = cache_v.reshape(-1, N_HEADS, HEAD_DIM)
    sel = in_past[..., None, None]
    k_sel = jnp.where(sel, flat_k[phys].astype(jnp.float32), gather_rows(k, nidx).astype(jnp.float32))
    v_sel = jnp.where(sel, flat_v[phys].astype(jnp.float32), gather_rows(v, nidx).astype(jnp.float32))
    return sparse_attend(q, k_sel, v_sel, valid)


def scan_combine(e1, e2):
    a1, b1 = e1
    a2, b2 = e2
    return a1 * a2, a2 * b1 + b2


def ssm_branch(u, h0, a_re, a_im, log_dt, b_re, b_im, c_re, c_im, d_skip, w_glu, b_glu):
    Bt, T = u.shape[:2]
    A = lax.complex(a_re.astype(jnp.float32), a_im.astype(jnp.float32))
    dtA = A * jnp.exp(log_dt.astype(jnp.float32))[:, None]
    a_bar = jnp.exp(dtA)
    Bm = lax.complex(b_re.astype(jnp.float32), b_im.astype(jnp.float32))
    b_bar = ((a_bar - 1.0) / A)[..., None] * Bm
    uf = u.astype(jnp.float32)
    bu = jnp.einsum('gpc,btgc->btgp', b_bar, uf.reshape(Bt, T, SSM_GROUPS, SSM_GROUP).astype(jnp.complex64))
    a_seq = jnp.broadcast_to(a_bar, bu.shape)
    _, h = lax.associative_scan(scan_combine, (a_seq, bu), axis=1)
    if h0 is not None:
        steps = jnp.arange(1, T + 1, dtype=jnp.float32)
        h = h + jnp.exp(steps[:, None, None] * dtA[None])[None] * h0[:, None]
    h_last = h[:, -1]
    Cm = lax.complex(c_re.astype(jnp.float32), c_im.astype(jnp.float32))
    y = jnp.real(jnp.einsum('gcp,btgp->btgc', Cm, h)).reshape(Bt, T, SSM_WIDTH) + d_skip.astype(jnp.float32) * uf
    y = jax.nn.gelu(y)
    y = y * jax.nn.sigmoid(y @ w_glu.astype(jnp.float32) + b_glu.astype(jnp.float32))
    return y.astype(u.dtype), h_last


def trunk_layer(x, p, pos, attend, h0, lw):
    h = rmsnorm(x, lw['norm_mix'])
    q, k, v, qi, ki, wi, u, ga, gb = mix_inputs(h, lw['w_in'], pos)
    attn = attend(q, k, v, qi, ki, wi)
    ssm, h_last = ssm_branch(u, h0, lw['a_re'], lw['a_im'], lw['log_dt'], lw['b_re'], lw['b_im'],
                             lw['c_re'], lw['c_im'], lw['d'], lw['w_glu'], lw['b_glu'])
    merged = jax.nn.sigmoid(ga) * (attn @ lw['w_branch_a']) + jax.nn.sigmoid(gb) * (ssm @ lw['w_branch_b'])
    x = x + merged @ lw['w_out']
    h = rmsnorm(x, lw['norm_ffn'])
    x = x + (jax.nn.silu(h @ lw['w_ff_gate']) * (h @ lw['w_ff_up'])) @ lw['w_ff_down']
    h = rmsnorm(x, lw['norm_ple'])
    x = x + jax.nn.sigmoid(h @ lw['w_ple_gate']) * (p @ lw['w_ple_proj'])
    return x, (k, v, ki, jnp.real(h_last), jnp.imag(h_last))


def setup_inputs(seed: int = 0) -> dict:
    key = jax.random.key(seed)
    ks = iter(jax.random.split(key, 40))
    nrm = lambda shape, scale=1.0: jax.random.normal(next(ks), shape, jnp.float32) * scale
    n_pages = PAST_LEN // PAGE_SIZE
    used = DEC_BATCH * n_pages
    n_phys = used + used // 4
    page_table = jax.random.permutation(next(ks), n_phys)[:used].reshape(DEC_BATCH, n_pages).astype(jnp.int32)
    log_dt = jax.random.uniform(next(ks), (DEPTH, SSM_GROUPS), jnp.float32, math.log(1e-3), math.log(1e-1))
    return {
        'x_prompt': nrm((BATCH, SEQ, D_MODEL)),
        'x_sample': nrm((DEC_BATCH, DEC_SEQ, D_MODEL)),
        'p_prompt': nrm((DEPTH, BATCH, SEQ, PLE_DIM)),
        'p_sample': nrm((DEPTH, DEC_BATCH, DEC_SEQ, PLE_DIM)),
        'cache_k': nrm((DEPTH, n_phys, PAGE_SIZE, N_HEADS, HEAD_DIM)),
        'cache_v': nrm((DEPTH, n_phys, PAGE_SIZE, N_HEADS, HEAD_DIM)),
        'cache_kidx': nrm((DEPTH, n_phys, PAGE_SIZE, IDX_DIM)),
        'state_ssm_re': nrm((DEPTH, DEC_BATCH, SSM_GROUPS, SSM_STATE), 0.5),
        'state_ssm_im': nrm((DEPTH, DEC_BATCH, SSM_GROUPS, SSM_STATE), 0.5),
        'page_table': page_table,
        'norm_mix': 1.0 + nrm((DEPTH, D_MODEL), 0.01),
        'w_in': nrm((DEPTH, D_MODEL, IN_WIDTH), D_MODEL ** -0.5),
        'w_branch_a': nrm((DEPTH, ATTN_WIDTH, D_MODEL), ATTN_WIDTH ** -0.5),
        'w_branch_b': nrm((DEPTH, SSM_WIDTH, D_MODEL), SSM_WIDTH ** -0.5),
        'w_out': nrm((DEPTH, D_MODEL, D_MODEL), D_MODEL ** -0.5),
        'ssm_a_re': -0.5 + nrm((DEPTH, SSM_GROUPS, SSM_STATE), 0.01),
        'ssm_a_im': math.pi * jnp.arange(SSM_STATE, dtype=jnp.float32) + nrm((DEPTH, SSM_GROUPS, SSM_STATE), 0.01),
        'ssm_log_dt': log_dt,
        'ssm_b_re': nrm((DEPTH, SSM_GROUPS, SSM_STATE, SSM_GROUP), (2 * SSM_GROUP) ** -0.5),
        'ssm_b_im': nrm((DEPTH, SSM_GROUPS, SSM_STATE, SSM_GROUP), (2 * SSM_GROUP) ** -0.5),
        'ssm_c_re': nrm((DEPTH, SSM_GROUPS, SSM_GROUP, SSM_STATE), (2 * SSM_STATE) ** -0.5),
        'ssm_c_im': nrm((DEPTH, SSM_GROUPS, SSM_GROUP, SSM_STATE), (2 * SSM_STATE) ** -0.5),
        'ssm_d': nrm((DEPTH, SSM_WIDTH)),
        'w_glu': nrm((DEPTH, SSM_WIDTH, SSM_WIDTH), SSM_WIDTH ** -0.5),
        'b_glu': nrm((DEPTH, SSM_WIDTH), 0.01),
        'norm_ffn': 1.0 + nrm((DEPTH, D_MODEL), 0.01),
        'w_ff_gate': nrm((DEPTH, D_MODEL, D_FF), D_MODEL ** -0.5),
        'w_ff_up': nrm((DEPTH, D_MODEL, D_FF), D_MODEL ** -0.5),
        'w_ff_down': nrm((DEPTH, D_FF, D_MODEL), D_FF ** -0.5),
        'norm_ple': 1.0 + nrm((DEPTH, D_MODEL), 0.01),
        'w_ple_gate': nrm((DEPTH, D_MODEL, D_MODEL), D_MODEL ** -0.5),
        'w_ple_proj': nrm((DEPTH, PLE_DIM, D_MODEL), PLE_DIM ** -0.5),
        'norm_final': 1.0 + nrm((D_MODEL,), 0.01),
    }


def reference(x_prompt, x_sample, p_prompt, p_sample, cache_k, cache_v, cache_kidx, state_ssm_re, state_ssm_im,
              page_table, norm_mix, w_in, w_branch_a, w_branch_b, w_out, ssm_a_re, ssm_a_im, ssm_log_dt,
              ssm_b_re, ssm_b_im, ssm_c_re, ssm_c_im, ssm_d, w_glu, b_glu, norm_ffn, w_ff_gate, w_ff_up,
              w_ff_down, norm_ple, w_ple_gate, w_ple_proj, norm_final):
    S = x_prompt.shape[1]
    T = x_sample.shape[1]
    past = page_table.shape[1] * PAGE_SIZE
    pos_p = jnp.arange(S)
    pos_s = past + jnp.arange(T)
    xp, xs = x_prompt, x_sample
    st_p, st_s = [], []
    for i in range(DEPTH):
        lw = {'norm_mix': norm_mix[i], 'w_in': w_in[i], 'w_branch_a': w_branch_a[i], 'w_branch_b': w_branch_b[i],
              'w_out': w_out[i], 'a_re': ssm_a_re[i], 'a_im': ssm_a_im[i], 'log_dt': ssm_log_dt[i],
              'b_re': ssm_b_re[i], 'b_im': ssm_b_im[i], 'c_re': ssm_c_re[i], 'c_im': ssm_c_im[i], 'd': ssm_d[i],
              'w_glu': w_glu[i], 'b_glu': b_glu[i], 'norm_ffn': norm_ffn[i], 'w_ff_gate': w_ff_gate[i],
              'w_ff_up': w_ff_up[i], 'w_ff_down': w_ff_down[i], 'norm_ple': norm_ple[i],
              'w_ple_gate': w_ple_gate[i], 'w_ple_proj': w_ple_proj[i]}
        ck, cv, cki = cache_k[i], cache_v[i], cache_kidx[i]

        def attend_sample(q, k, v, qi, ki, wi, ck=ck, cv=cv, cki=cki):
            return sample_attention(q, k, v, qi, ki, wi, ck, cv, cki, page_table)

        h0 = lax.complex(state_ssm_re[i].astype(jnp.float32), state_ssm_im[i].astype(jnp.float32))
        xp, sp = trunk_layer(xp, p_prompt[i], pos_p, prompt_attention, None, lw)
        xs, ss = trunk_layer(xs, p_sample[i], pos_s, attend_sample, h0, lw)
        st_p.append(sp)
        st_s.append(ss)
    y_prompt = rmsnorm(xp, norm_final)
    y_sample = rmsnorm(xs, norm_final)
    k_prompt = jnp.stack([s[0] for s in st_p])
    v_prompt = jnp.stack([s[1] for s in st_p])
    kidx_prompt = jnp.stack([s[2] for s in st_p])
    ssm_re_prompt = jnp.stack([s[3] for s in st_p])
    ssm_im_prompt = jnp.stack([s[4] for s in st_p])
    k_sample = jnp.stack([s[0] for s in st_s])
    v_sample = jnp.stack([s[1] for s in st_s])
    kidx_sample = jnp.stack([s[2] for s in st_s])
    ssm_re_sample = jnp.stack([s[3] for s in st_s])
    ssm_im_sample = jnp.stack([s[4] for s in st_s])
    return (y_prompt, y_sample, k_prompt, v_prompt, kidx_prompt, ssm_re_prompt, ssm_im_prompt,
            k_sample, v_sample, kidx_sample, ssm_re_sample, ssm_im_sample)
```

```python
import functools
import math

import numpy as np
import jax
import jax.numpy as jnp
from jax import lax
from jax.experimental import pallas as pl
from jax.experimental.pallas import tpu as pltpu

N_HEADS = 8
HEAD_DIM = 64
ATTN_WIDTH = N_HEADS * HEAD_DIM
IDX_HEADS = 8
IDX_DIM = 64
TOPK_MAX = 256
ROPE_THETA = 500000.0
ROPE_ROT = HEAD_DIM // 4
ROPE_HALF = ROPE_ROT // 2
SSM_GROUP = 16
SSM_GROUPS = 32
SSM_WIDTH = SSM_GROUP * SSM_GROUPS
SSM_STATE = 64
SSM_COLS = SSM_GROUPS * SSM_STATE
PAGE_SIZE = 128
Q_BLOCK = 128
EPS = 1e-6
NEG = -1e30

LANES = 128
SUBLANES = 8
VMEM_LIMIT = 56 * 1024 * 1024

MXU_DTYPE = jnp.bfloat16
F32 = jnp.float32
I32 = jnp.int32
INT_MIN = -(2 ** 31)


def _key_to_f32(key):
    return pltpu.bitcast(key ^ ((key >> 31) & 0x7FFFFFFF), F32)


def _kth_largest(count_ge, n_sel, shape):
    zero = jnp.zeros(shape, I32)
    base = jnp.where(count_ge(_key_to_f32(zero), zero) >= n_sel, 0, INT_MIN).astype(I32)

    def bit_step(i, base):
        cand = base | jnp.left_shift(jnp.int32(1), 30 - i)
        return jnp.where(count_ge(_key_to_f32(cand), cand) >= n_sel, cand, base)

    return _key_to_f32(lax.fori_loop(0, 31, bit_step, base))


def _cparams(sem):
    return pltpu.CompilerParams(dimension_semantics=sem, vmem_limit_bytes=VMEM_LIMIT)


def _nt_dot(a, b):
    return lax.dot_general(a, b, (((1,), (1,)), ((), ())), preferred_element_type=F32)


def _rope_table_kernel(invf_ref, c_ref, s1_ref, s2_ref, *, pos0, period, rows):
    i = pl.program_id(0)
    r = lax.broadcasted_iota(I32, (rows, LANES), 0) + i * rows
    pos = (pos0 + r % period).astype(F32)
    ang = pos * invf_ref[...]
    lane = lax.broadcasted_iota(I32, (rows, LANES), 1) % HEAD_DIM
    c = jnp.cos(ang)
    s = jnp.sin(ang)
    c_ref[...] = jnp.where(lane < ROPE_ROT, c, 1.0)
    s1_ref[...] = jnp.where(lane < ROPE_HALF, -s, 0.0)
    s2_ref[...] = jnp.where((lane >= ROPE_HALF) & (lane < ROPE_ROT), s, 0.0)


def _rope_tables(n_rows, pos0, period):
    inv_freq = ROPE_THETA ** (-jnp.arange(0, ROPE_ROT, 2, dtype=F32) / ROPE_ROT)
    lane = np.arange(LANES) % HEAD_DIM
    invf = jnp.where(lane < ROPE_ROT, inv_freq[lane % ROPE_HALF], 0.0).astype(F32)[None, :]
    rows = min(n_rows, 512)
    assert n_rows % rows == 0
    spec = pl.BlockSpec((rows, LANES), lambda i: (i, 0))
    return pl.pallas_call(
        functools.partial(_rope_table_kernel, pos0=pos0, period=period, rows=rows),
        grid=(n_rows // rows,),
        in_specs=[pl.BlockSpec((1, LANES), lambda i: (0, 0))],
        out_specs=[spec, spec, spec],
        out_shape=[jax.ShapeDtypeStruct((n_rows, LANES), F32)] * 3,
        compiler_params=_cparams(("arbitrary",)),
        name="rope_tables",
    )(invf)


W_Q, W_K, W_V, W_QI, W_KIWI, W_U = 0, 512, 1024, 1536, 2048, 2176
W_MAIN = 2688


def _inproj_kernel(x_ref, g_ref, w_ref, c_ref, s1_ref, s2_ref,
                   q_ref, kbf_ref, kt_ref, vbf_ref, vt_ref, vtbf_ref, qi_ref,
                   kiwi_ref, kiwit_ref, kidup_ref, u_ref):
    x = x_ref[...]
    ms = jnp.mean(x * x, axis=-1, keepdims=True)
    h = (x * lax.rsqrt(ms + EPS) * g_ref[...]).astype(MXU_DTYPE)
    cos_t, sin_lo, sin_hi = c_ref[...], s1_ref[...], s2_ref[...]

    def proj(lo, hi):
        return jnp.dot(h, w_ref[:, lo:hi], preferred_element_type=F32)

    def rope_tile(t):
        return t * cos_t + pltpu.roll(t, LANES - ROPE_HALF, 1) * sin_lo + pltpu.roll(t, ROPE_HALF, 1) * sin_hi

    def rope(z):
        return jnp.concatenate([rope_tile(z[:, j * LANES:(j + 1) * LANES]) for j in range(z.shape[1] // LANES)], axis=1)

    q_ref[...] = (rope(proj(W_Q, W_K)) * (HEAD_DIM ** -0.5)).astype(q_ref.dtype)
    k = rope(proj(W_K, W_V))
    kbf_ref[...] = k.astype(kbf_ref.dtype)
    kt_ref[...] = k.T
    v = proj(W_V, W_QI)
    vbf_ref[...] = v.astype(vbf_ref.dtype)
    vt = v.T
    vt_ref[...] = vt
    vtbf_ref[...] = vt.astype(vtbf_ref.dtype)
    qi_ref[...] = (rope(proj(W_QI, W_KIWI)) * (IDX_DIM ** -0.5)).astype(qi_ref.dtype)
    z = proj(W_KIWI, W_U)
    lane = lax.broadcasted_iota(I32, z.shape, 1)
    kiwi = jnp.where(lane < IDX_DIM, rope_tile(z), z * (IDX_HEADS ** -0.5))
    kiwi_ref[...] = kiwi
    kiwit_ref[...] = kiwi.T
    kidup_ref[...] = jnp.where(lane < IDX_DIM, kiwi, pltpu.roll(kiwi, IDX_DIM, 1)).astype(kidup_ref.dtype)
    u_ref[...] = proj(W_U, W_MAIN)


def _inproj(x, gain, w_main, tables, tm):
    bk, sk, d = x.shape
    nt = sk // tm
    assert sk % tm == 0
    cos_t, sin_lo, sin_hi = tables
    n_tab = cos_t.shape[0] // tm
    rows = bk * sk
    xf = x.reshape(rows, d)
    row = lambda w: pl.BlockSpec((tm, w), lambda i: (i, 0))
    tr = lambda w: pl.BlockSpec((None, w, tm), lambda i: (i // nt, 0, i % nt))
    tab = pl.BlockSpec((tm, LANES), lambda i: (i % n_tab, 0))
    outs = [
        (row(ATTN_WIDTH), (rows, ATTN_WIDTH), MXU_DTYPE),
        (row(ATTN_WIDTH), (rows, ATTN_WIDTH), MXU_DTYPE),
        (tr(ATTN_WIDTH), (bk, ATTN_WIDTH, sk), F32),
        (row(ATTN_WIDTH), (rows, ATTN_WIDTH), MXU_DTYPE),
        (tr(ATTN_WIDTH), (bk, ATTN_WIDTH, sk), F32),
        (tr(ATTN_WIDTH), (bk, ATTN_WIDTH, sk), MXU_DTYPE),
        (row(ATTN_WIDTH), (rows, ATTN_WIDTH), MXU_DTYPE),
        (row(LANES), (rows, LANES), F32),
        (tr(LANES), (bk, LANES, sk), F32),
        (row(LANES), (rows, LANES), MXU_DTYPE),
        (row(SSM_WIDTH), (rows, SSM_WIDTH), F32),
    ]
    return pl.pallas_call(
        _inproj_kernel,
        grid=(rows // tm,),
        in_specs=[row(d),
                  pl.BlockSpec((1, d), lambda i: (0, 0)),
                  pl.BlockSpec((d, W_MAIN), lambda i: (0, 0)),
                  tab, tab, tab],
        out_specs=[o[0] for o in outs],
        out_shape=[jax.ShapeDtypeStruct(o[1], o[2]) for o in outs],
        compiler_params=_cparams(("arbitrary",)),
        name="inproj",
    )(xf, gain, w_main, cos_t, sin_lo, sin_hi)


KEY_CHUNK = 512


def _head_masked_pair(tile, lane_lo):
    zero = jnp.zeros_like(tile)
    return jnp.concatenate([jnp.where(lane_lo, tile, zero), jnp.where(lane_lo, zero, tile)], axis=0)


def _prompt_attn_kernel(q_ref, qi_ref, kiwit_ref, k_ref, kidup_ref, vt_ref, o_ref,
                        sc_ref, lg_ref, ot_ref, rhs_ref, tie_ref, *, seq, n_sel):
    qb = pl.program_id(1)
    n_keys = (qb + 1) * Q_BLOCK
    nch = (n_keys + KEY_CHUNK - 1) // KEY_CHUNK
    n_virtual = seq - nch * KEY_CHUNK
    qpos = qb * Q_BLOCK + lax.broadcasted_iota(I32, (1, Q_BLOCK), 1)
    lane_lo = lax.broadcasted_iota(I32, (Q_BLOCK, LANES), 1) < HEAD_DIM
    row_iota = lax.broadcasted_iota(I32, (KEY_CHUNK, Q_BLOCK), 0)
    n_pairs = N_HEADS // 2
    groups = KEY_CHUNK // SUBLANES

    def chunk_rows(c):
        return pl.ds(pl.multiple_of(c * KEY_CHUNK, KEY_CHUNK), KEY_CHUNK)

    def group_reduce(op, a):
        return op(a.reshape(groups, SUBLANES, a.shape[1]), axis=0)

    for p in range(n_pairs):
        rhs_ref[p] = _head_masked_pair(qi_ref[:, p * LANES:(p + 1) * LANES], lane_lo)
        rhs_ref[n_pairs + p] = _head_masked_pair(q_ref[:, p * LANES:(p + 1) * LANES], lane_lo)
    wi_t = kiwit_ref[IDX_DIM:IDX_DIM + IDX_HEADS, :]

    def score_chunk(c, carry):
        rows = chunk_rows(c)
        kd = kidup_ref[rows, :]
        acc = jnp.zeros((KEY_CHUNK, Q_BLOCK), F32)
        for p in range(n_pairs):
            s = _nt_dot(kd, rhs_ref[p])
            acc = acc + jnp.maximum(s[:, :Q_BLOCK], 0.0) * wi_t[2 * p:2 * p + 1, :]
            acc = acc + jnp.maximum(s[:, Q_BLOCK:], 0.0) * wi_t[2 * p + 1:2 * p + 2, :]
        visible = (row_iota + c * KEY_CHUNK) <= qpos
        sc_ref[rows, :] = jnp.where(visible, acc, NEG)
        for p in range(n_pairs):
            lg_ref[p, rows, :] = _nt_dot(k_ref[rows, p * LANES:(p + 1) * LANES], rhs_ref[n_pairs + p])
        return carry

    lax.fori_loop(0, nch, score_chunk, 0)

    def count(pred):
        def body(c, acc):
            m = pred(sc_ref[chunk_rows(c), :], c * KEY_CHUNK).astype(I32)
            return acc + group_reduce(jnp.sum, m)
        acc = lax.fori_loop(0, nch, body, jnp.zeros((SUBLANES, Q_BLOCK), I32))
        return jnp.sum(acc, axis=0, keepdims=True)

    def count_ge(cand, key):
        return count(lambda sc, off: sc >= cand) + jnp.where(cand <= NEG, n_virtual, 0)

    thr = _kth_largest(count_ge, n_sel, (1, Q_BLOCK))
    cnt_ge = count_ge(thr, None)
    cnt_gt = count(lambda sc, off: sc > thr) + jnp.where(thr < NEG, n_virtual, 0)
    need = n_sel - cnt_gt

    idx_bits = max(1, (seq - 1).bit_length())
    tie_ref[...] = jnp.full((1, Q_BLOCK), (1 << idx_bits) - 1, I32)

    @pl.when(jnp.max(cnt_ge) > n_sel)
    def _():
        def idx_step(i, x):
            cand = x | jnp.left_shift(jnp.int32(1), idx_bits - 1 - i)
            c = count(lambda kk, off: (kk == thr) & ((row_iota + off) < cand))
            return jnp.where(c < need, cand, x)
        tie_ref[...] = lax.fori_loop(0, idx_bits, idx_step, jnp.zeros((1, Q_BLOCK), I32))

    tie_x = tie_ref[...]

    def mask_chunk(c, ms):
        rows = chunk_rows(c)
        sc = sc_ref[rows, :]
        idx = row_iota + c * KEY_CHUNK
        taken = jnp.where(sc > thr, 1, jnp.where(sc == thr, jnp.where(idx <= tie_x, 1, 0), 0))
        sel = jnp.where(idx <= qpos, taken, 0) > 0
        out = []
        for p in range(n_pairs):
            lo = jnp.where(sel, lg_ref[p, rows, :Q_BLOCK], NEG)
            hi = jnp.where(sel, lg_ref[p, rows, Q_BLOCK:], NEG)
            lg_ref[p, rows, :Q_BLOCK] = lo
            lg_ref[p, rows, Q_BLOCK:] = hi
            out.append(jnp.maximum(ms[2 * p], group_reduce(jnp.max, lo)))
            out.append(jnp.maximum(ms[2 * p + 1], group_reduce(jnp.max, hi)))
        return tuple(out)

    ms = lax.fori_loop(0, nch, mask_chunk, (jnp.full((SUBLANES, Q_BLOCK), -jnp.inf, F32),) * N_HEADS)
    ms = [jnp.max(m, axis=0, keepdims=True) for m in ms]

    ot_ref[...] = jnp.zeros_like(ot_ref)

    def pv_chunk(c, dens):
        rows = chunk_rows(c)
        out = []
        for h in range(N_HEADS):
            half = slice((h % 2) * Q_BLOCK, (h % 2 + 1) * Q_BLOCK)
            e = jnp.exp(lg_ref[h // 2, rows, half] - ms[h])
            out.append(dens[h] + group_reduce(jnp.sum, e))
            v = vt_ref[h * HEAD_DIM:(h + 1) * HEAD_DIM, rows]
            ot_ref[h * HEAD_DIM:(h + 1) * HEAD_DIM, :] += jnp.dot(v, e.astype(MXU_DTYPE), preferred_element_type=F32)
        return tuple(out)

    dens = lax.fori_loop(0, nch, pv_chunk, (jnp.zeros((SUBLANES, Q_BLOCK), F32),) * N_HEADS)
    for h in range(N_HEADS):
        den = jnp.sum(dens[h], axis=0, keepdims=True)
        ot_ref[h * HEAD_DIM:(h + 1) * HEAD_DIM, :] = ot_ref[h * HEAD_DIM:(h + 1) * HEAD_DIM, :] / den
    o_ref[...] = ot_ref[...].T.astype(o_ref.dtype)


def _prompt_attention(q, qi, kiwit, k, kidup, vt, batch, seq):
    nqb = seq // Q_BLOCK
    assert seq % KEY_CHUNK == 0
    n_sel = min(TOPK_MAX, seq // 4)
    qspec = pl.BlockSpec((Q_BLOCK, ATTN_WIDTH), lambda b, j: (b * nqb + j, 0))
    return pl.pallas_call(
        functools.partial(_prompt_attn_kernel, seq=seq, n_sel=n_sel),
        grid=(batch, nqb),
        in_specs=[qspec, qspec,
                  pl.BlockSpec((None, LANES, Q_BLOCK), lambda b, j: (b, 0, j)),
                  pl.BlockSpec((seq, ATTN_WIDTH), lambda b, j: (b, 0)),
                  pl.BlockSpec((seq, LANES), lambda b, j: (b, 0)),
                  pl.BlockSpec((None, ATTN_WIDTH, seq), lambda b, j: (b, 0, 0))],
        out_specs=qspec,
        out_shape=jax.ShapeDtypeStruct((batch * seq, ATTN_WIDTH), MXU_DTYPE),
        scratch_shapes=[pltpu.VMEM((seq, Q_BLOCK), F32),
                        pltpu.VMEM((N_HEADS // 2, seq, 2 * Q_BLOCK), F32),
                        pltpu.VMEM((ATTN_WIDTH, Q_BLOCK), F32),
                        pltpu.VMEM((N_HEADS, 2 * Q_BLOCK, LANES), MXU_DTYPE),
                        pltpu.VMEM((1, Q_BLOCK), I32)],
        compiler_params=_cparams(("arbitrary", "arbitrary")),
        name="prompt_attention",
    )(q, qi, kiwit, k, kidup, vt)


def _ssm_disc_kernel(are_ref, aim_ref, ldt_ref, bre_ref, bim_ref, abr_ref, abi_ref, bbr_ref, bbi_ref):
    are, aim = are_ref[...], aim_ref[...]
    dt = jnp.exp(ldt_ref[...])
    mag = jnp.exp(are * dt)
    abr = mag * jnp.cos(aim * dt)
    abi = mag * jnp.sin(aim * dt)
    abr_ref[...] = abr
    abi_ref[...] = abi
    nr, ni = abr - 1.0, abi
    den = are * are + aim * aim
    fr = (nr * are + ni * aim) / den
    fi = (ni * are - nr * aim) / den
    bre, bim = bre_ref[...], bim_ref[...]
    bbr_ref[...] = fr * bre - fi * bim
    bbi_ref[...] = fr * bim + fi * bre


def _ssm_discretize(a_re, a_im, log_dt, b_re, b_im):
    col = lambda a: a.reshape(SSM_COLS, 1)
    ldt = jnp.broadcast_to(log_dt[:, None], (SSM_GROUPS, SSM_STATE))
    flat = lambda b: b.reshape(SSM_COLS, SSM_GROUP)
    shp1 = jax.ShapeDtypeStruct((SSM_COLS, 1), F32)
    shpb = jax.ShapeDtypeStruct((SSM_COLS, SSM_GROUP), F32)
    return pl.pallas_call(_ssm_disc_kernel, out_shape=[shp1, shp1, shpb, shpb], name="ssm_discretize")(
        col(a_re), col(a_im), col(ldt), flat(b_re), flat(b_im))


SSM_CBLK = LANES // SSM_GROUP
SSM_NCB = SSM_WIDTH // LANES
SSM_TILES = SSM_COLS // LANES
SSM_CB_COLS = SSM_CBLK * SSM_STATE


def _ssm_kernel(u_ref, h0r_ref, h0i_ref, ar_ref, ai_ref, bre_ref, bim_ref, cre_ref, cim_ref, d_ref,
                y_ref, hr_ref, hi_ref, bu_ref, hs_ref, st_ref, *, tc, n_pass):
    ci = pl.program_id(1)
    nb = u_ref.shape[0]
    rows = nb * tc

    @pl.when(ci == 0)
    def _():
        for j in range(SSM_TILES):
            st_ref[j] = h0r_ref[:, j * LANES:(j + 1) * LANES]
            st_ref[SSM_TILES + j] = h0i_ref[:, j * LANES:(j + 1) * LANES]

    u = pltpu.einshape("btc->tbc", u_ref[...]).reshape(rows, SSM_WIDTH)
    ub = u.astype(MXU_DTYPE)
    per_cb = SSM_CB_COLS // LANES
    for cb in range(SSM_NCB):
        ucb = ub[:, cb * LANES:(cb + 1) * LANES]
        re = jnp.dot(ucb, bre_ref[cb], preferred_element_type=F32)
        im = jnp.dot(ucb, bim_ref[cb], preferred_element_type=F32)
        for jj in range(per_cb):
            bu_ref[cb * per_cb + jj] = re[:, jj * LANES:(jj + 1) * LANES]
            bu_ref[SSM_TILES + cb * per_cb + jj] = im[:, jj * LANES:(jj + 1) * LANES]

    per_pass = SSM_TILES // n_pass
    for ps in range(n_pass):
        tiles = range(ps * per_pass, (ps + 1) * per_pass)
        ar = [ar_ref[:, j * LANES:(j + 1) * LANES] for j in tiles]
        ai = [ai_ref[:, j * LANES:(j + 1) * LANES] for j in tiles]

        def step(t, carry, tiles=tiles, ar=ar, ai=ai):
            out = []
            rows_t = pl.ds(pl.multiple_of(t * nb, nb), nb)
            for n, j in enumerate(tiles):
                hr, hi = carry[2 * n], carry[2 * n + 1]
                nr = ar[n] * hr - ai[n] * hi + bu_ref[j, rows_t, :]
                ni = ar[n] * hi + ai[n] * hr + bu_ref[SSM_TILES + j, rows_t, :]
                hs_ref[j, rows_t, :] = nr
                hs_ref[SSM_TILES + j, rows_t, :] = ni
                out += [nr, ni]
            return tuple(out)

        init = []
        for j in tiles:
            init += [st_ref[j], st_ref[SSM_TILES + j]]
        fin = lax.fori_loop(0, tc, step, tuple(init), unroll=2)
        for n, j in enumerate(tiles):
            st_ref[j] = fin[2 * n]
            st_ref[SSM_TILES + j] = fin[2 * n + 1]

    for cb in range(SSM_NCB):
        tiles = range(cb * per_cb, (cb + 1) * per_cb)
        h_re = jnp.concatenate([hs_ref[j] for j in tiles], axis=1).astype(MXU_DTYPE)
        h_im = jnp.concatenate([hs_ref[SSM_TILES + j] for j in tiles], axis=1).astype(MXU_DTYPE)
        acc = (d_ref[:, cb * LANES:(cb + 1) * LANES] * u[:, cb * LANES:(cb + 1) * LANES]
               + jnp.dot(h_re, cre_ref[cb], preferred_element_type=F32)
               + jnp.dot(h_im, cim_ref[cb], preferred_element_type=F32))
        y_ref[:, :, cb * LANES:(cb + 1) * LANES] = pltpu.einshape("tbc->btc", acc.reshape(tc, nb, LANES))

    @pl.when(ci == pl.num_programs(1) - 1)
    def _():
        for j in range(SSM_TILES):
            hr_ref[:, j * LANES:(j + 1) * LANES] = st_ref[j]
            hi_ref[:, j * LANES:(j + 1) * LANES] = st_ref[SSM_TILES + j]


def _ssm_block_weights(bbar_re, bbar_im, c_re, c_im):
    eye = jnp.eye(SSM_CBLK, dtype=F32)

    def b_blk(b):
        bt = b.reshape(SSM_NCB, SSM_CBLK, SSM_STATE, SSM_GROUP).transpose(0, 1, 3, 2)
        return (bt[:, :, :, None, :] * eye[None, :, None, :, None]).reshape(SSM_NCB, LANES, SSM_CB_COLS).astype(MXU_DTYPE)

    def c_blk(c):
        ct = c.reshape(SSM_NCB, SSM_CBLK, SSM_GROUP, SSM_STATE).transpose(0, 1, 3, 2)
        return (ct[:, :, :, None, :] * eye[None, :, None, :, None]).reshape(SSM_NCB, SSM_CB_COLS, LANES).astype(MXU_DTYPE)

    return b_blk(bbar_re), b_blk(bbar_im), c_blk(c_re), c_blk(-c_im)


def _ssm_scan(u, h0_re, h0_im, abar_re, abar_im, blocks, d_skip, tc, n_pass):
    nb_tot, t_tot, _ = u.shape
    nb = SUBLANES
    assert nb_tot % nb == 0 and t_tot % tc == 0 and tc % SUBLANES == 0
    bre, bim, cre, cim = blocks
    rows = nb * tc
    cst = lambda shape: pl.BlockSpec(shape, lambda g, c: (0,) * len(shape))
    st_spec = pl.BlockSpec((nb, SSM_COLS), lambda g, c: (g, 0))
    u_spec = pl.BlockSpec((nb, tc, SSM_WIDTH), lambda g, c: (g, c, 0))
    return pl.pallas_call(
        functools.partial(_ssm_kernel, tc=tc, n_pass=n_pass),
        grid=(nb_tot // nb, t_tot // tc),
        in_specs=[u_spec, st_spec, st_spec, cst((1, SSM_COLS)), cst((1, SSM_COLS)),
                  cst(bre.shape), cst(bim.shape), cst(cre.shape), cst(cim.shape), cst((1, SSM_WIDTH))],
        out_specs=[u_spec, st_spec, st_spec],
        out_shape=[jax.ShapeDtypeStruct(u.shape, F32),
                   jax.ShapeDtypeStruct((nb_tot, SSM_COLS), F32),
                   jax.ShapeDtypeStruct((nb_tot, SSM_COLS), F32)],
        scratch_shapes=[pltpu.VMEM((2 * SSM_TILES, rows, LANES), F32),
                        pltpu.VMEM((2 * SSM_TILES, rows, LANES), F32),
                        pltpu.VMEM((2 * SSM_TILES, nb, LANES), F32)],
        compiler_params=_cparams(("arbitrary", "arbitrary")),
        name="ssm_scan",
    )(u, h0_re, h0_im, abar_re, abar_im, bre, bim, cre, cim, d_skip)


def _rms(x, g):
    return x * lax.rsqrt(jnp.mean(x * x, axis=-1, keepdims=True) + EPS) * g


def _mxu(a, w_ref):
    return jnp.dot(a.astype(MXU_DTYPE), w_ref[...], preferred_element_type=F32)


def _tail_kernel(x_ref, attn_ref, ypre_ref, p_ref, gmix_ref, gffn_ref, gple_ref, gfin_ref,
                 wga_ref, wgb_ref, wa_ref, wglu_ref, bglu_ref, wb_ref, wout_ref,
                 wfg_ref, wfu_ref, wfd_ref, wpg_ref, wpp_ref, o_ref, *, ff_chunks, final_norm):
    x = x_ref[...]
    h = _rms(x, gmix_ref[...]).astype(MXU_DTYPE)
    branch_a = _mxu(attn_ref[...], wa_ref)
    y = jax.nn.gelu(ypre_ref[...])
    y = y * jax.nn.sigmoid(_mxu(y, wglu_ref) + bglu_ref[...])
    branch_b = _mxu(y, wb_ref)
    merged = (jax.nn.sigmoid(jnp.dot(h, wga_ref[...], preferred_element_type=F32)) * branch_a
              + jax.nn.sigmoid(jnp.dot(h, wgb_ref[...], preferred_element_type=F32)) * branch_b)
    x = x + _mxu(merged, wout_ref)
    h = _rms(x, gffn_ref[...]).astype(MXU_DTYPE)
    d_ff = wfg_ref.shape[1]
    fc = d_ff // ff_chunks
    ffn = jnp.zeros_like(x)
    for c in range(ff_chunks):
        gate = jnp.dot(h, wfg_ref[:, c * fc:(c + 1) * fc], preferred_element_type=F32)
        up = jnp.dot(h, wfu_ref[:, c * fc:(c + 1) * fc], preferred_element_type=F32)
        act = (jax.nn.silu(gate) * up).astype(MXU_DTYPE)
        ffn = ffn + jnp.dot(act, wfd_ref[c * fc:(c + 1) * fc, :], preferred_element_type=F32)
    x = x + ffn
    h = _rms(x, gple_ref[...])
    x = x + jax.nn.sigmoid(_mxu(h, wpg_ref)) * _mxu(p_ref[...], wpp_ref)
    o_ref[...] = _rms(x, gfin_ref[...]) if final_norm else x


def _tail(x, attn, ypre, p, gains, weights, tm, final_norm, ff_chunks=2):
    rows, d = x.shape
    assert rows % tm == 0
    row = lambda a: pl.BlockSpec((tm, a.shape[1]), lambda i: (i, 0))
    cst = lambda a: pl.BlockSpec(a.shape, lambda i: (0, 0), pipeline_mode=pl.Buffered(1))
    return pl.pallas_call(
        functools.partial(_tail_kernel, ff_chunks=ff_chunks, final_norm=final_norm),
        grid=(rows // tm,),
        in_specs=[row(x), row(attn), row(ypre), row(p)] + [cst(g) for g in gains] + [cst(w) for w in weights],
        out_specs=row(x),
        out_shape=jax.ShapeDtypeStruct(x.shape, F32),
        compiler_params=_cparams(("arbitrary",)),
        name="merge_ffn_ple",
    )(x, attn, ypre, p, *gains, *weights)


SCORE_CHUNK = 1024
KV_RING = 16
PAGE_UNROLL = 4


def _sample_attn_kernel(pt_ref, q_ref, qi_ref, kn_ref, vn_ref, kiwi_ref, kidx_hbm, ck_hbm, cv_hbm, o_ref,
                        kib_ref, kv_ref, sc_ref, sel_ref, lg_ref, acc_ref, tie_ref, kisem, kvsem,
                        *, group, t_new, n_pages, n_sel, n_steps):
    step = pl.program_id(0)
    past = n_pages * PAGE_SIZE
    width = past + LANES
    rows = group * t_new
    hq = N_HEADS * t_new
    chunks_per_seq = 2 * n_pages
    total_chunks = n_steps * group * chunks_per_seq
    lookahead = KV_RING - 1

    def ki_copy(s, g, page):
        src = kidx_hbm.at[pt_ref[(s * group + g) * n_pages + page]]
        dst = kib_ref.at[s % 2, g, :, pl.ds(pl.multiple_of(page * PAGE_SIZE, PAGE_SIZE), PAGE_SIZE)]
        return pltpu.make_async_copy(src, dst, kisem.at[s % 2])

    def for_ki_pages(s, fn):
        for g in range(group):
            def body(page, c, g=g):
                fn(ki_copy(s, g, page))
                return c
            lax.fori_loop(0, n_pages, body, 0)

    def kv_copy(gc, src_hbm):
        seq = gc // chunks_per_seq
        page = gc % n_pages
        return pltpu.make_async_copy(src_hbm.at[pt_ref[seq * n_pages + page]], kv_ref.at[gc % KV_RING],
                                     kvsem.at[gc % KV_RING])

    def issue_kv(gc):
        is_k = (gc % chunks_per_seq) < n_pages

        @pl.when((gc < total_chunks) & is_k)
        def _():
            kv_copy(gc, ck_hbm).start()

        @pl.when((gc < total_chunks) & jnp.logical_not(is_k))
        def _():
            kv_copy(gc, cv_hbm).start()

    @pl.when(step == 0)
    def _():
        for_ki_pages(step, lambda cp: cp.start())
        for c in range(lookahead):
            issue_kv(jnp.int32(c))

    for_ki_pages(step, lambda cp: cp.wait())

    @pl.when(step + 1 < n_steps)
    def _():
        for_ki_pages(step + 1, lambda cp: cp.start())

    lane1 = lax.broadcasted_iota(I32, (t_new, LANES), 1)
    row1 = lax.broadcasted_iota(I32, (t_new, LANES), 0)
    new_visible = lane1 <= row1

    def pad_rows(a):
        return jnp.concatenate([a, jnp.zeros((LANES - t_new, a.shape[1]), a.dtype)], axis=0)

    for g in range(group):
        r0 = g * t_new
        qi_g = qi_ref[r0:r0 + t_new, :].astype(MXU_DTYPE)
        qi_hq = jnp.concatenate([qi_g[:, h * IDX_DIM:(h + 1) * IDX_DIM] for h in range(IDX_HEADS)], axis=0)
        kiwi_g = kiwi_ref[r0:r0 + t_new, :]
        w_col = [kiwi_g[:, IDX_DIM + h:IDX_DIM + h + 1] for h in range(IDX_HEADS)]

        def head_sum(s):
            acc = jnp.zeros((t_new, s.shape[1]), F32)
            for h in range(IDX_HEADS):
                acc = acc + jnp.maximum(s[h * t_new:(h + 1) * t_new, :], 0.0) * w_col[h]
            return acc

        def score_chunk(c, carry, g=g, r0=r0, qi_hq=qi_hq, head_sum=head_sum):
            off = pl.multiple_of(c * SCORE_CHUNK, SCORE_CHUNK)
            kc = kib_ref[step % 2, g, :, pl.ds(off, SCORE_CHUNK)].astype(MXU_DTYPE)
            s = jnp.dot(qi_hq, kc, preferred_element_type=F32)
            sc_ref[r0:r0 + t_new, pl.ds(off, SCORE_CHUNK)] = head_sum(s)
            return carry

        lax.fori_loop(0, past // SCORE_CHUNK, score_chunk, 0)
        ki_new = pad_rows(kiwi_g[:, :IDX_DIM]).astype(MXU_DTYPE)
        s_new = jnp.where(new_visible, head_sum(_nt_dot(qi_hq, ki_new)), NEG)
        sc_ref[r0:r0 + t_new, past:width] = jnp.where(lane1 < t_new, s_new, -jnp.inf)

    n_tiles = width // LANES

    def count(pred):
        acc = jnp.zeros((rows, LANES), I32)
        for j in range(n_tiles):
            acc = acc + pred(sc_ref[:, j * LANES:(j + 1) * LANES], j * LANES).astype(I32)
        return jnp.sum(acc, axis=1, keepdims=True)

    thr = _kth_largest(lambda cand, key: count(lambda kk, off: kk >= cand), n_sel, (rows, 1))
    cnt_ge = count(lambda kk, off: kk >= thr)
    need = n_sel - count(lambda kk, off: kk > thr)

    idx_bits = (width - 1).bit_length()
    lane_r = lax.broadcasted_iota(I32, (rows, LANES), 1)
    tie_ref[...] = jnp.full((rows, 1), (1 << idx_bits) - 1, I32)

    @pl.when(jnp.max(cnt_ge) > n_sel)
    def _():
        def idx_step(i, x):
            cand = x | jnp.left_shift(jnp.int32(1), idx_bits - 1 - i)
            c = count(lambda kk, off: (kk == thr) & ((lane_r + off) < cand))
            return jnp.where(c < need, cand, x)
        tie_ref[...] = lax.fori_loop(0, idx_bits, idx_step, jnp.zeros((rows, 1), I32))

    tie_x = tie_ref[...]
    new_vis_rows = jnp.concatenate([new_visible] * group, axis=0)
    for j in range(n_tiles):
        kk = sc_ref[:, j * LANES:(j + 1) * LANES]
        taken = jnp.where(kk > thr, 1, jnp.where(kk == thr, jnp.where((lane_r + j * LANES) <= tie_x, 1, 0), 0))
        if j == n_tiles - 1:
            taken = jnp.where(new_vis_rows, taken, 0)
        sel_ref[:, j * LANES:(j + 1) * LANES] = taken.astype(F32)

    head_of_row = lax.broadcasted_iota(I32, (hq, ATTN_WIDTH), 0) // t_new
    head_of_lane = lax.broadcasted_iota(I32, (hq, ATTN_WIDTH), 1) // HEAD_DIM
    for g in range(group):
        r0 = g * t_new
        chunk0 = (step * group + g) * chunks_per_seq
        q_g = q_ref[r0:r0 + t_new, :]
        q_bd = jnp.where(head_of_row == head_of_lane, jnp.concatenate([q_g] * N_HEADS, axis=0), 0.0).astype(MXU_DTYPE)

        def sel_hq(lo, r0=r0):
            s = sel_ref[r0:r0 + t_new, pl.ds(lo, LANES)]
            return jnp.concatenate([s] * N_HEADS, axis=0) > 0.0

        def k_pages(it, m, chunk0=chunk0, q_bd=q_bd, sel_hq=sel_hq):
            gc0 = chunk0 + it * PAGE_UNROLL
            for un in range(PAGE_UNROLL):
                kv_copy(gc0 + un, ck_hbm).wait()
            for un in range(PAGE_UNROLL):
                kp = kv_ref[(gc0 + un) % KV_RING].astype(MXU_DTYPE)
                off = pl.multiple_of((it * PAGE_UNROLL + un) * PAGE_SIZE, PAGE_SIZE)
                lg = jnp.where(sel_hq(off), jnp.dot(q_bd, kp, preferred_element_type=F32), NEG)
                lg_ref[:, pl.ds(off, LANES)] = lg
                m = jnp.maximum(m, lg)
            for un in range(PAGE_UNROLL):
                issue_kv(gc0 + un + lookahead)
            return m

        m = lax.fori_loop(0, n_pages // PAGE_UNROLL, k_pages, jnp.full((hq, LANES), -jnp.inf, F32))
        k_new = pad_rows(kn_ref[r0:r0 + t_new, :]).astype(MXU_DTYPE)
        lg_new = jnp.where(sel_hq(past), _nt_dot(q_bd, k_new), NEG)
        lg_ref[:, past:width] = lg_new
        m = jnp.max(jnp.maximum(m, lg_new), axis=1, keepdims=True)

        acc_ref[...] = jnp.zeros_like(acc_ref)

        def v_pages(it, den, chunk0=chunk0, m=m):
            gc0 = chunk0 + n_pages + it * PAGE_UNROLL
            for un in range(PAGE_UNROLL):
                kv_copy(gc0 + un, ck_hbm).wait()
            acc = acc_ref[...]
            for un in range(PAGE_UNROLL):
                vp = kv_ref[(gc0 + un) % KV_RING].astype(MXU_DTYPE)
                off = pl.multiple_of((it * PAGE_UNROLL + un) * PAGE_SIZE, PAGE_SIZE)
                e = jnp.exp(lg_ref[:, pl.ds(off, LANES)] - m)
                acc = acc + _nt_dot(e.astype(MXU_DTYPE), vp)
                den = den + e
            acc_ref[...] = acc
            for un in range(PAGE_UNROLL):
                issue_kv(gc0 + un + lookahead)
            return den

        den = lax.fori_loop(0, n_pages // PAGE_UNROLL, v_pages, jnp.zeros((hq, LANES), F32))
        e_new = jnp.exp(lg_ref[:, past:width] - m)
        v_new = pad_rows(vn_ref[r0:r0 + t_new, :]).astype(MXU_DTYPE)
        acc = acc_ref[...] + jnp.dot(e_new.astype(MXU_DTYPE), v_new, preferred_element_type=F32)
        o = acc / jnp.sum(den + e_new, axis=1, keepdims=True)
        o = jnp.where(head_of_row == head_of_lane, o, 0.0)
        out = o[0:t_new, :]
        for h in range(1, N_HEADS):
            out = out + o[h * t_new:(h + 1) * t_new, :]
        o_ref[r0:r0 + t_new, :] = out


def _sample_attention(q, qi, k_new, v_new, kiwi, kidx_pages, k_pages, v_pages, page_table, t_new, group):
    rows_tot = q.shape[0]
    n_seq, n_pages = page_table.shape
    assert n_seq % group == 0 and (n_pages * PAGE_SIZE) % SCORE_CHUNK == 0 and t_new <= SUBLANES
    assert n_pages % PAGE_UNROLL == 0
    n_steps = n_seq // group
    rows = group * t_new
    past = n_pages * PAGE_SIZE
    width = past + LANES
    n_sel = min(TOPK_MAX, (past + t_new) // 4)
    row = lambda w: pl.BlockSpec((rows, w), lambda i, pt: (i, 0))
    hbm = pl.BlockSpec(memory_space=pl.ANY)
    grid_spec = pltpu.PrefetchScalarGridSpec(
        num_scalar_prefetch=1,
        grid=(n_steps,),
        in_specs=[row(ATTN_WIDTH), row(ATTN_WIDTH), row(ATTN_WIDTH), row(ATTN_WIDTH), row(LANES), hbm, hbm, hbm],
        out_specs=row(ATTN_WIDTH),
        scratch_shapes=[pltpu.VMEM((2, group, IDX_DIM, past), F32),
                        pltpu.VMEM((KV_RING, ATTN_WIDTH, PAGE_SIZE), F32),
                        pltpu.VMEM((rows, width), F32),
                        pltpu.VMEM((rows, width), F32),
                        pltpu.VMEM((N_HEADS * t_new, width), F32),
                        pltpu.VMEM((N_HEADS * t_new, ATTN_WIDTH), F32),
                        pltpu.VMEM((rows, 1), I32),
                        pltpu.SemaphoreType.DMA((2,)),
                        pltpu.SemaphoreType.DMA((KV_RING,))])
    return pl.pallas_call(
        functools.partial(_sample_attn_kernel, group=group, t_new=t_new, n_pages=n_pages, n_sel=n_sel,
                          n_steps=n_steps),
        grid_spec=grid_spec,
        out_shape=jax.ShapeDtypeStruct((rows_tot, ATTN_WIDTH), F32),
        compiler_params=_cparams(("arbitrary",)),
        name="sample_attention",
    )(page_table.reshape(-1), q, qi, k_new, v_new, kiwi, kidx_pages, k_pages, v_pages)


IN_SIZES = (ATTN_WIDTH, ATTN_WIDTH, ATTN_WIDTH, IDX_HEADS * IDX_DIM, IDX_DIM, IDX_HEADS, SSM_WIDTH)
PROMPT_ROW_TILE = 512
SAMPLE_GROUP = 4
SSM_TIME_CHUNK = 128
SSM_PASSES = 2


def _split_w_in(w):
    d = w.shape[0]
    offs = np.cumsum((0,) + IN_SIZES)
    o_ki, o_u, o_ga = int(offs[4]), int(offs[6]), int(offs[7])
    kiwi = jnp.pad(w[:, o_ki:o_u], ((0, 0), (0, LANES - (o_u - o_ki))))
    w_main = jnp.concatenate([w[:, :o_ki], kiwi, w[:, o_u:o_ga]], axis=1).astype(MXU_DTYPE)
    return w_main, w[:, o_ga:o_ga + d].astype(MXU_DTYPE), w[:, o_ga + d:o_ga + 2 * d].astype(MXU_DTYPE)


def kernel(x_prompt, x_sample, p_prompt, p_sample, cache_k, cache_v, cache_kidx, state_ssm_re, state_ssm_im, page_table, norm_mix, w_in, w_branch_a, w_branch_b, w_out, ssm_a_re, ssm_a_im, ssm_log_dt, ssm_b_re, ssm_b_im, ssm_c_re, ssm_c_im, ssm_d, w_glu, b_glu, norm_ffn, w_ff_gate, w_ff_up, w_ff_down, norm_ple, w_ple_gate, w_ple_proj, norm_final):
    b, s, d = x_prompt.shape
    db, t, _ = x_sample.shape
    depth = w_in.shape[0]
    n_pages = page_table.shape[1]
    past = n_pages * PAGE_SIZE
    n_phys = cache_k.shape[1]
    rows_p, rows_s = b * s, db * t
    tm_p = min(PROMPT_ROW_TILE, s)
    tm_s = min(PROMPT_ROW_TILE, rows_s)
    tables_p = _rope_tables(s, 0, s)
    tables_s = _rope_tables(tm_s, past, t)
    bf = lambda a: a.astype(MXU_DTYPE)
    row = lambda a: a.reshape(1, -1)

    xp = x_prompt.reshape(rows_p, d)
    xs = x_sample.reshape(rows_s, d)
    st_p, st_s = [], []
    for i in range(depth):
        w_main, w_ga, w_gb = _split_w_in(w_in[i])
        abr, abi, bbr, bbi = _ssm_discretize(ssm_a_re[i], ssm_a_im[i], ssm_log_dt[i], ssm_b_re[i], ssm_b_im[i])
        blocks = _ssm_block_weights(bbr, bbi, ssm_c_re[i], ssm_c_im[i])
        gains = [row(norm_mix[i]), row(norm_ffn[i]), row(norm_ple[i]), row(norm_final)]
        weights = [w_ga, w_gb, bf(w_branch_a[i]), bf(w_glu[i]), row(b_glu[i]), bf(w_branch_b[i]), bf(w_out[i]),
                   bf(w_ff_gate[i]), bf(w_ff_up[i]), bf(w_ff_down[i]), bf(w_ple_gate[i]), bf(w_ple_proj[i])]
        final = i == depth - 1

        (q, kbf, kt, _, vt, vtbf, qi, _, kiwit, kidup, u) = _inproj(
            xp.reshape(b, s, d), gains[0], w_main, tables_p, tm_p)
        attn = _prompt_attention(q, qi, kiwit, kbf, kidup, vtbf, b, s)
        zeros = jnp.zeros((b, SSM_COLS), F32)
        ypre, hr, hi = _ssm_scan(u.reshape(b, s, SSM_WIDTH), zeros, zeros, row(abr), row(abi), blocks,
                                 row(ssm_d[i]), min(SSM_TIME_CHUNK, s), SSM_PASSES)
        xp = _tail(xp, attn, ypre.reshape(rows_p, SSM_WIDTH), p_prompt[i].reshape(rows_p, -1), gains, weights,
                   tm_p, final)
        heads = lambda a: a.reshape(b, N_HEADS, HEAD_DIM, s).transpose(0, 3, 1, 2)
        st_p.append((heads(kt), heads(vt), kiwit[:, :IDX_DIM, :].transpose(0, 2, 1),
                     hr.reshape(b, SSM_GROUPS, SSM_STATE), hi.reshape(b, SSM_GROUPS, SSM_STATE)))

        (q, kbf, kt, vbf, vt, _, qi, kiwi, _, _, u) = _inproj(
            xs.reshape(1, rows_s, d), gains[0], w_main, tables_s, tm_s)
        f32 = lambda a: a.astype(F32)
        attn = _sample_attention(
            f32(q), f32(qi), f32(kbf), f32(vbf), kiwi,
            cache_kidx[i].transpose(0, 2, 1),
            cache_k[i].transpose(0, 2, 3, 1).reshape(n_phys, ATTN_WIDTH, PAGE_SIZE),
            cache_v[i].transpose(0, 2, 3, 1).reshape(n_phys, ATTN_WIDTH, PAGE_SIZE),
            page_table, t, SAMPLE_GROUP)
        ypre, hr, hi = _ssm_scan(u.reshape(db, t, SSM_WIDTH), state_ssm_re[i].reshape(db, SSM_COLS),
                                 state_ssm_im[i].reshape(db, SSM_COLS), row(abr), row(abi), blocks,
                                 row(ssm_d[i]), t, SSM_PASSES)
        xs = _tail(xs, attn, ypre.reshape(rows_s, SSM_WIDTH), p_sample[i].reshape(rows_s, -1), gains, weights,
                   tm_s, final)
        heads = lambda a: a.reshape(N_HEADS, HEAD_DIM, db, t).transpose(2, 3, 0, 1)
        st_s.append((heads(kt), heads(vt), kiwi[:, :IDX_DIM].reshape(db, t, IDX_DIM),
                     hr.reshape(db, SSM_GROUPS, SSM_STATE), hi.reshape(db, SSM_GROUPS, SSM_STATE)))

    stack = lambda sts, j: jnp.stack([st[j] for st in sts])
    return (xp.reshape(b, s, d), xs.reshape(db, t, d),
            *[stack(st_p, j) for j in range(5)], *[stack(st_s, j) for j in range(5)])
```

```python
import functools
import math

import numpy as np
import jax
import jax.numpy as jnp
from jax import lax
from jax.experimental import pallas as pl
from jax.experimental.pallas import tpu as pltpu

N_HEADS = 8
HEAD_DIM = 64
ATTN_WIDTH = N_HEADS * HEAD_DIM
IDX_HEADS = 8
IDX_DIM = 64
TOPK_MAX = 256
ROPE_THETA = 500000.0
ROPE_ROT = HEAD_DIM // 4
ROPE_HALF = ROPE_ROT // 2
SSM_GROUP = 16
SSM_GROUPS = 32
SSM_WIDTH = SSM_GROUP * SSM_GROUPS
SSM_STATE = 64
SSM_COLS = SSM_GROUPS * SSM_STATE
PAGE_SIZE = 128
Q_BLOCK = 128
EPS = 1e-6
NEG = -1e30

LANES = 128
SUBLANES = 8
VMEM_LIMIT = 56 * 1024 * 1024

MXU_DTYPE = jnp.bfloat16
F32 = jnp.float32
I32 = jnp.int32
INT_MIN = -(2 ** 31)


def _key_to_f32(key):
    return pltpu.bitcast(key ^ ((key >> 31) & 0x7FFFFFFF), F32)


def _kth_largest(count_ge, n_sel, shape):
    zero = jnp.zeros(shape, I32)
    base = jnp.where(count_ge(_key_to_f32(zero), zero) >= n_sel, 0, INT_MIN).astype(I32)

    def bit_step(i, base):
        cand = base | jnp.left_shift(jnp.int32(1), 30 - i)
        return jnp.where(count_ge(_key_to_f32(cand), cand) >= n_sel, cand, base)

    return _key_to_f32(lax.fori_loop(0, 31, bit_step, base))


def _cparams(sem):
    return pltpu.CompilerParams(dimension_semantics=sem, vmem_limit_bytes=VMEM_LIMIT)


def _nt_dot(a, b):
    return lax.dot_general(a, b, (((1,), (1,)), ((), ())), preferred_element_type=F32)


def _rope_table_kernel(invf_ref, c_ref, s1_ref, s2_ref, *, pos0, period, rows):
    i = pl.program_id(0)
    r = lax.broadcasted_iota(I32, (rows, LANES), 0) + i * rows
    pos = (pos0 + r % period).astype(F32)
    ang = pos * invf_ref[...]
    lane = lax.broadcasted_iota(I32, (rows, LANES), 1) % HEAD_DIM
    c = jnp.cos(ang)
    s = jnp.sin(ang)
    c_ref[...] = jnp.where(lane < ROPE_ROT, c, 1.0)
    s1_ref[...] = jnp.where(lane < ROPE_HALF, -s, 0.0)
    s2_ref[...] = jnp.where((lane >= ROPE_HALF) & (lane < ROPE_ROT), s, 0.0)


def _rope_tables(n_rows, pos0, period):
    inv_freq = ROPE_THETA ** (-jnp.arange(0, ROPE_ROT, 2, dtype=F32) / ROPE_ROT)
    lane = np.arange(LANES) % HEAD_DIM
    invf = jnp.where(lane < ROPE_ROT, inv_freq[lane % ROPE_HALF], 0.0).astype(F32)[None, :]
    rows = min(n_rows, 512)
    assert n_rows % rows == 0
    spec = pl.BlockSpec((rows, LANES), lambda i: (i, 0))
    return pl.pallas_call(
        functools.partial(_rope_table_kernel, pos0=pos0, period=period, rows=rows),
        grid=(n_rows // rows,),
        in_specs=[pl.BlockSpec((1, LANES), lambda i: (0, 0))],
        out_specs=[spec, spec, spec],
        out_shape=[jax.ShapeDtypeStruct((n_rows, LANES), F32)] * 3,
        compiler_params=_cparams(("arbitrary",)),
        name="rope_tables",
    )(invf)


W_Q, W_K, W_V, W_QI, W_KIWI, W_U = 0, 512, 1024, 1536, 2048, 2176
W_MAIN = 2688


def _inproj_kernel(x_ref, g_ref, w_ref, c_ref, s1_ref, s2_ref,
                   q_ref, kbf_ref, kt_ref, vbf_ref, vt_ref, vtbf_ref, qi_ref,
                   kiwi_ref, kiwit_ref, kidup_ref, u_ref):
    x = x_ref[...]
    ms = jnp.mean(x * x, axis=-1, keepdims=True)
    h = (x * lax.rsqrt(ms + EPS) * g_ref[...]).astype(MXU_DTYPE)
    cos_t, sin_lo, sin_hi = c_ref[...], s1_ref[...], s2_ref[...]

    def proj(lo, hi):
        return jnp.dot(h, w_ref[:, lo:hi], preferred_element_type=F32)

    def rope_tile(t):
        return t * cos_t + pltpu.roll(t, LANES - ROPE_HALF, 1) * sin_lo + pltpu.roll(t, ROPE_HALF, 1) * sin_hi

    def rope(z):
        return jnp.concatenate([rope_tile(z[:, j * LANES:(j + 1) * LANES]) for j in range(z.shape[1] // LANES)], axis=1)

    q_ref[...] = (rope(proj(W_Q, W_K)) * (HEAD_DIM ** -0.5)).astype(q_ref.dtype)
    k = rope(proj(W_K, W_V))
    kbf_ref[...] = k.astype(kbf_ref.dtype)
    kt_ref[...] = k.T
    v = proj(W_V, W_QI)
    vbf_ref[...] = v.astype(vbf_ref.dtype)
    vt = v.T
    vt_ref[...] = vt
    vtbf_ref[...] = vt.astype(vtbf_ref.dtype)
    qi_ref[...] = (rope(proj(W_QI, W_KIWI)) * (IDX_DIM ** -0.5)).astype(qi_ref.dtype)
    z = proj(W_KIWI, W_U)
    lane = lax.broadcasted_iota(I32, z.shape, 1)
    kiwi = jnp.where(lane < IDX_DIM, rope_tile(z), z * (IDX_HEADS ** -0.5))
    kiwi_ref[...] = kiwi
    kiwit_ref[...] = kiwi.T
    kidup_ref[...] = jnp.where(lane < IDX_DIM, kiwi, pltpu.roll(kiwi, IDX_DIM, 1)).astype(kidup_ref.dtype)
    u_ref[...] = proj(W_U, W_MAIN)


def _inproj(x, gain, w_main, tables, tm):
    bk, sk, d = x.shape
    nt = sk // tm
    assert sk % tm == 0
    cos_t, sin_lo, sin_hi = tables
    n_tab = cos_t.shape[0] // tm
    rows = bk * sk
    xf = x.reshape(rows, d)
    row = lambda w: pl.BlockSpec((tm, w), lambda i: (i, 0))
    tr = lambda w: pl.BlockSpec((None, w, tm), lambda i: (i // nt, 0, i % nt))
    tab = pl.BlockSpec((tm, LANES), lambda i: (i % n_tab, 0))
    outs = [
        (row(ATTN_WIDTH), (rows, ATTN_WIDTH), MXU_DTYPE),
        (row(ATTN_WIDTH), (rows, ATTN_WIDTH), MXU_DTYPE),
        (tr(ATTN_WIDTH), (bk, ATTN_WIDTH, sk), F32),
        (row(ATTN_WIDTH), (rows, ATTN_WIDTH), MXU_DTYPE),
        (tr(ATTN_WIDTH), (bk, ATTN_WIDTH, sk), F32),
        (tr(ATTN_WIDTH), (bk, ATTN_WIDTH, sk), MXU_DTYPE),
        (row(ATTN_WIDTH), (rows, ATTN_WIDTH), MXU_DTYPE),
        (row(LANES), (rows, LANES), F32),
        (tr(LANES), (bk, LANES, sk), F32),
        (row(LANES), (rows, LANES), MXU_DTYPE),
        (row(SSM_WIDTH), (rows, SSM_WIDTH), F32),
    ]
    return pl.pallas_call(
        _inproj_kernel,
        grid=(rows // tm,),
        in_specs=[row(d),
                  pl.BlockSpec((1, d), lambda i: (0, 0)),
                  pl.BlockSpec((d, W_MAIN), lambda i: (0, 0)),
                  tab, tab, tab],
        out_specs=[o[0] for o in outs],
        out_shape=[jax.ShapeDtypeStruct(o[1], o[2]) for o in outs],
        compiler_params=_cparams(("arbitrary",)),
        name="inproj",
    )(xf, gain, w_main, cos_t, sin_lo, sin_hi)


KEY_CHUNK = 512
VT_ONES_ROWS = 16


def _head_masked_pair(tile, lane_lo):
    zero = jnp.zeros_like(tile)
    return jnp.concatenate([jnp.where(lane_lo, tile, zero), jnp.where(lane_lo, zero, tile)], axis=0)


def _prompt_attn_kernel(q_ref, qi_ref, kw_ref, k_ref, kidup_ref, vt_ref, o_ref,
                        sc_ref, lg_ref, ot_ref, rhs_ref, wi_ref, tie_ref, *, seq, n_sel):
    qb = pl.program_id(1)
    lane_lo = lax.broadcasted_iota(I32, (Q_BLOCK, LANES), 1) < HEAD_DIM
    lane_q = lax.broadcasted_iota(I32, (1, Q_BLOCK), 1)
    row_iota = lax.broadcasted_iota(I32, (KEY_CHUNK, Q_BLOCK), 0)
    n_pairs = N_HEADS // 2
    groups = KEY_CHUNK // SUBLANES

    def n_chunks(blk):
        return ((blk + 1) * Q_BLOCK + KEY_CHUNK - 1) // KEY_CHUNK

    nch = n_chunks(qb)
    n_virtual = seq - nch * KEY_CHUNK
    qpos = qb * Q_BLOCK + lane_q

    def chunk_rows(c):
        return pl.ds(pl.multiple_of(c * KEY_CHUNK, KEY_CHUNK), KEY_CHUNK)

    def group_reduce(op, a):
        return op(a.reshape(groups, SUBLANES, a.shape[1]), axis=0)

    for p in range(n_pairs):
        rhs_ref[p] = _head_masked_pair(qi_ref[:, p * LANES:(p + 1) * LANES], lane_lo)
        rhs_ref[n_pairs + p] = _head_masked_pair(q_ref[:, p * LANES:(p + 1) * LANES], lane_lo)
    wi_ref[...] = kw_ref[IDX_DIM:IDX_DIM + IDX_HEADS, :]

    def produce_chunk(c, carry):
        rows = chunk_rows(c)
        kd = kidup_ref[rows, :]
        acc = jnp.zeros((KEY_CHUNK, Q_BLOCK), F32)
        for p in range(n_pairs):
            s = _nt_dot(kd, rhs_ref[p])
            acc = acc + jnp.maximum(s[:, :Q_BLOCK], 0.0) * wi_ref[2 * p:2 * p + 1, :]
            acc = acc + jnp.maximum(s[:, Q_BLOCK:], 0.0) * wi_ref[2 * p + 1:2 * p + 2, :]
        visible = (row_iota + c * KEY_CHUNK) <= qpos
        sc_ref[rows, :] = jnp.where(visible, acc, NEG)
        for p in range(n_pairs):
            lg = _nt_dot(k_ref[rows, p * LANES:(p + 1) * LANES], rhs_ref[n_pairs + p])
            lg_ref[p, rows, :] = lg.astype(lg_ref.dtype)
        return carry

    lax.fori_loop(0, nch, produce_chunk, 0)

    def count(pred):
        def body(c, acc):
            m = pred(sc_ref[chunk_rows(c), :], c * KEY_CHUNK).astype(I32)
            return acc + group_reduce(jnp.sum, m)
        acc = lax.fori_loop(0, nch, body, jnp.zeros((SUBLANES, Q_BLOCK), I32))
        return jnp.sum(acc, axis=0, keepdims=True)

    def count_ge(cand, key):
        return count(lambda sc, off: sc >= cand) + jnp.where(cand <= NEG, n_virtual, 0)

    thr = _kth_largest(count_ge, n_sel, (1, Q_BLOCK))
    cnt_ge = count_ge(thr, None)
    cnt_gt = count(lambda sc, off: sc > thr) + jnp.where(thr < NEG, n_virtual, 0)
    need = n_sel - cnt_gt

    idx_bits = max(1, (seq - 1).bit_length())
    tie_ref[...] = jnp.full((1, Q_BLOCK), (1 << idx_bits) - 1, I32)

    @pl.when(jnp.max(cnt_ge) > n_sel)
    def _():
        def idx_step(i, x):
            cand = x | jnp.left_shift(jnp.int32(1), idx_bits - 1 - i)
            c = count(lambda kk, off: (kk == thr) & ((row_iota + off) < cand))
            return jnp.where(c < need, cand, x)
        tie_ref[...] = lax.fori_loop(0, idx_bits, idx_step, jnp.zeros((1, Q_BLOCK), I32))

    tie_x = tie_ref[...]

    ldt = lg_ref.dtype
    pack = SUBLANES * 4 // jnp.dtype(ldt).itemsize
    neg_l = jnp.asarray(NEG, ldt)

    def mask_chunk(c, ms):
        rows = chunk_rows(c)
        sc = sc_ref[rows, :]
        idx = row_iota + c * KEY_CHUNK
        taken = jnp.where(sc > thr, 1, jnp.where(sc == thr, jnp.where(idx <= tie_x, 1, 0), 0))
        bias = jnp.where(jnp.where(idx <= qpos, taken, 0) > 0, 0.0, NEG).astype(ldt)
        dropped = bias < 0
        out = []
        for h in range(N_HEADS):
            half = slice((h % 2) * Q_BLOCK, (h % 2 + 1) * Q_BLOCK)
            lg = jnp.where(dropped, neg_l, lg_ref[h // 2, rows, half])
            lg_ref[h // 2, rows, half] = lg
            out.append(jnp.maximum(ms[h], jnp.max(lg.reshape(KEY_CHUNK // pack, pack, Q_BLOCK), axis=0)))
        return tuple(out)

    ms = lax.fori_loop(0, nch, mask_chunk, (jnp.full((pack, Q_BLOCK), -jnp.inf, ldt),) * N_HEADS)
    ms = [jnp.max(m.astype(F32), axis=0, keepdims=True).astype(ldt) for m in ms]

    ot_ref[...] = jnp.zeros_like(ot_ref)
    aug = vt_ref.shape[0] // N_HEADS

    def pv_chunk(c, carry):
        rows = chunk_rows(c)
        for h in range(N_HEADS):
            half = slice((h % 2) * Q_BLOCK, (h % 2 + 1) * Q_BLOCK)
            hrows = slice(h * aug, (h + 1) * aug)
            e = jnp.exp(lg_ref[h // 2, rows, half] - ms[h]).astype(MXU_DTYPE)
            ot_ref[hrows, :] += jnp.dot(vt_ref[hrows, rows], e, preferred_element_type=F32)
        return carry

    lax.fori_loop(0, nch, pv_chunk, 0)
    outs = []
    for h in range(N_HEADS):
        blk = ot_ref[h * aug:(h + 1) * aug, :]
        outs.append(blk[:HEAD_DIM, :] / blk[HEAD_DIM:HEAD_DIM + 1, :])
    o_ref[...] = jnp.concatenate(outs, axis=0).T.astype(o_ref.dtype)


def _prompt_attention(q, qi, kiwit, k, kidup, vt, batch, seq):
    nqb = seq // Q_BLOCK
    assert seq % KEY_CHUNK == 0
    n_sel = min(TOPK_MAX, seq // 4)
    qspec = pl.BlockSpec((Q_BLOCK, ATTN_WIDTH), lambda b, j: (b * nqb + j, 0))
    ones = jnp.ones((batch, N_HEADS, VT_ONES_ROWS, seq), vt.dtype)
    vt_aug = jnp.concatenate([vt.reshape(batch, N_HEADS, HEAD_DIM, seq), ones], axis=2)
    aug_rows = N_HEADS * (HEAD_DIM + VT_ONES_ROWS)
    return pl.pallas_call(
        functools.partial(_prompt_attn_kernel, seq=seq, n_sel=n_sel),
        grid=(batch, nqb),
        in_specs=[qspec, qspec,
                  pl.BlockSpec((None, LANES, Q_BLOCK), lambda b, j: (b, 0, j)),
                  pl.BlockSpec((seq, ATTN_WIDTH), lambda b, j: (b, 0)),
                  pl.BlockSpec((seq, LANES), lambda b, j: (b, 0)),
                  pl.BlockSpec((None, aug_rows, seq), lambda b, j: (b, 0, 0))],
        out_specs=qspec,
        out_shape=jax.ShapeDtypeStruct((batch * seq, ATTN_WIDTH), MXU_DTYPE),
        scratch_shapes=[pltpu.VMEM((seq, Q_BLOCK), F32),
                        pltpu.VMEM((N_HEADS // 2, seq, 2 * Q_BLOCK), MXU_DTYPE),
                        pltpu.VMEM((aug_rows, Q_BLOCK), F32),
                        pltpu.VMEM((N_HEADS, 2 * Q_BLOCK, LANES), MXU_DTYPE),
                        pltpu.VMEM((IDX_HEADS, Q_BLOCK), F32),
                        pltpu.VMEM((1, Q_BLOCK), I32)],
        compiler_params=_cparams(("arbitrary", "arbitrary")),
        name="prompt_attention",
    )(q, qi, kiwit, k, kidup, vt_aug.reshape(batch, aug_rows, seq))


def _ssm_disc_kernel(are_ref, aim_ref, ldt_ref, bre_ref, bim_ref, abr_ref, abi_ref, bbr_ref, bbi_ref):
    are, aim = are_ref[...], aim_ref[...]
    dt = jnp.exp(ldt_ref[...])
    mag = jnp.exp(are * dt)
    abr = mag * jnp.cos(aim * dt)
    abi = mag * jnp.sin(aim * dt)
    abr_ref[...] = abr
    abi_ref[...] = abi
    nr, ni = abr - 1.0, abi
    den = are * are + aim * aim
    fr = (nr * are + ni * aim) / den
    fi = (ni * are - nr * aim) / den
    bre, bim = bre_ref[...], bim_ref[...]
    bbr_ref[...] = fr * bre - fi * bim
    bbi_ref[...] = fr * bim + fi * bre


def _ssm_discretize(a_re, a_im, log_dt, b_re, b_im):
    col = lambda a: a.reshape(SSM_COLS, 1)
    ldt = jnp.broadcast_to(log_dt[:, None], (SSM_GROUPS, SSM_STATE))
    flat = lambda b: b.reshape(SSM_COLS, SSM_GROUP)
    shp1 = jax.ShapeDtypeStruct((SSM_COLS, 1), F32)
    shpb = jax.ShapeDtypeStruct((SSM_COLS, SSM_GROUP), F32)
    return pl.pallas_call(_ssm_disc_kernel, out_shape=[shp1, shp1, shpb, shpb], name="ssm_discretize")(
        col(a_re), col(a_im), col(ldt), flat(b_re), flat(b_im))


SSM_CBLK = LANES // SSM_GROUP
SSM_NCB = SSM_WIDTH // LANES
SSM_TILES = SSM_COLS // LANES
SSM_CB_COLS = SSM_CBLK * SSM_STATE


def _ssm_kernel(u_ref, h0r_ref, h0i_ref, ar_ref, ai_ref, bre_ref, bim_ref, cre_ref, cim_ref, d_ref,
                y_ref, hr_ref, hi_ref, bu_ref, hs_ref, st_ref, *, tc, n_pass):
    ci = pl.program_id(1)
    nb = u_ref.shape[0]
    rows = nb * tc

    @pl.when(ci == 0)
    def _():
        for j in range(SSM_TILES):
            st_ref[j] = h0r_ref[:, j * LANES:(j + 1) * LANES]
            st_ref[SSM_TILES + j] = h0i_ref[:, j * LANES:(j + 1) * LANES]

    u = jnp.swapaxes(u_ref[...], 0, 1).reshape(rows, SSM_WIDTH)
    ub = u.astype(MXU_DTYPE)
    per_cb = SSM_CB_COLS // LANES
    for cb in range(SSM_NCB):
        ucb = ub[:, cb * LANES:(cb + 1) * LANES]
        re = jnp.dot(ucb, bre_ref[cb], preferred_element_type=F32)
        im = jnp.dot(ucb, bim_ref[cb], preferred_element_type=F32)
        for jj in range(per_cb):
            bu_ref[cb * per_cb + jj] = re[:, jj * LANES:(jj + 1) * LANES]
            bu_ref[SSM_TILES + cb * per_cb + jj] = im[:, jj * LANES:(jj + 1) * LANES]

    per_pass = SSM_TILES // n_pass
    for ps in range(n_pass):
        tiles = range(ps * per_pass, (ps + 1) * per_pass)
        ar = [ar_ref[:, j * LANES:(j + 1) * LANES] for j in tiles]
        ai = [ai_ref[:, j * LANES:(j + 1) * LANES] for j in tiles]

        def step(t, carry, tiles=tiles, ar=ar, ai=ai):
            out = []
            rows_t = pl.ds(pl.multiple_of(t * nb, nb), nb)
            for n, j in enumerate(tiles):
                hr, hi = carry[2 * n], carry[2 * n + 1]
                nr = ar[n] * hr - ai[n] * hi + bu_ref[j, rows_t, :]
                ni = ar[n] * hi + ai[n] * hr + bu_ref[SSM_TILES + j, rows_t, :]
                hs_ref[j, rows_t, :] = nr
                hs_ref[SSM_TILES + j, rows_t, :] = ni
                out += [nr, ni]
            return tuple(out)

        init = []
        for j in tiles:
            init += [st_ref[j], st_ref[SSM_TILES + j]]
        fin = lax.fori_loop(0, tc, step, tuple(init), unroll=2)
        for n, j in enumerate(tiles):
            st_ref[j] = fin[2 * n]
            st_ref[SSM_TILES + j] = fin[2 * n + 1]

    for cb in range(SSM_NCB):
        tiles = range(cb * per_cb, (cb + 1) * per_cb)
        h_re = jnp.concatenate([hs_ref[j] for j in tiles], axis=1).astype(MXU_DTYPE)
        h_im = jnp.concatenate([hs_ref[SSM_TILES + j] for j in tiles], axis=1).astype(MXU_DTYPE)
        acc = (d_ref[:, cb * LANES:(cb + 1) * LANES] * u[:, cb * LANES:(cb + 1) * LANES]
               + jnp.dot(h_re, cre_ref[cb], preferred_element_type=F32)
               + jnp.dot(h_im, cim_ref[cb], preferred_element_type=F32))
        y_ref[:, :, cb * LANES:(cb + 1) * LANES] = jnp.swapaxes(acc.reshape(tc, nb, LANES), 0, 1)

    @pl.when(ci == pl.num_programs(1) - 1)
    def _():
        for j in range(SSM_TILES):
            hr_ref[:, j * LANES:(j + 1) * LANES] = st_ref[j]
            hi_ref[:, j * LANES:(j + 1) * LANES] = st_ref[SSM_TILES + j]


def _ssm_block_weights(bbar_re, bbar_im, c_re, c_im):
    eye = jnp.eye(SSM_CBLK, dtype=F32)

    def b_blk(b):
        bt = b.reshape(SSM_NCB, SSM_CBLK, SSM_STATE, SSM_GROUP).transpose(0, 1, 3, 2)
        return (bt[:, :, :, None, :] * eye[None, :, None, :, None]).reshape(SSM_NCB, LANES, SSM_CB_COLS).astype(MXU_DTYPE)

    def c_blk(c):
        ct = c.reshape(SSM_NCB, SSM_CBLK, SSM_GROUP, SSM_STATE).transpose(0, 1, 3, 2)
        return (ct[:, :, :, None, :] * eye[None, :, None, :, None]).reshape(SSM_NCB, SSM_CB_COLS, LANES).astype(MXU_DTYPE)

    return b_blk(bbar_re), b_blk(bbar_im), c_blk(c_re), c_blk(-c_im)


def _ssm_scan(u, h0_re, h0_im, abar_re, abar_im, blocks, d_skip, tc, n_pass):
    nb_tot, t_tot, _ = u.shape
    nb = SUBLANES
    assert nb_tot % nb == 0 and t_tot % tc == 0 and tc % SUBLANES == 0
    bre, bim, cre, cim = blocks
    rows = nb * tc
    cst = lambda shape: pl.BlockSpec(shape, lambda g, c: (0,) * len(shape))
    st_spec = pl.BlockSpec((nb, SSM_COLS), lambda g, c: (g, 0))
    u_spec = pl.BlockSpec((nb, tc, SSM_WIDTH), lambda g, c: (g, c, 0))
    return pl.pallas_call(
        functools.partial(_ssm_kernel, tc=tc, n_pass=n_pass),
        grid=(nb_tot // nb, t_tot // tc),
        in_specs=[u_spec, st_spec, st_spec, cst((1, SSM_COLS)), cst((1, SSM_COLS)),
                  cst(bre.shape), cst(bim.shape), cst(cre.shape), cst(cim.shape), cst((1, SSM_WIDTH))],
        out_specs=[u_spec, st_spec, st_spec],
        out_shape=[jax.ShapeDtypeStruct(u.shape, F32),
                   jax.ShapeDtypeStruct((nb_tot, SSM_COLS), F32),
                   jax.ShapeDtypeStruct((nb_tot, SSM_COLS), F32)],
        scratch_shapes=[pltpu.VMEM((2 * SSM_TILES, rows, LANES), F32),
                        pltpu.VMEM((2 * SSM_TILES, rows, LANES), F32),
                        pltpu.VMEM((2 * SSM_TILES, nb, LANES), F32)],
        compiler_params=_cparams(("arbitrary", "arbitrary")),
        name="ssm_scan",
    )(u, h0_re, h0_im, abar_re, abar_im, bre, bim, cre, cim, d_skip)


def _rms(x, g):
    return x * lax.rsqrt(jnp.mean(x * x, axis=-1, keepdims=True) + EPS) * g


def _mxu(a, w_ref):
    return jnp.dot(a.astype(MXU_DTYPE), w_ref[...], preferred_element_type=F32)


def _tail_kernel(x_ref, attn_ref, ypre_ref, p_ref, gmix_ref, gffn_ref, gple_ref, gfin_ref,
                 wga_ref, wgb_ref, wa_ref, wglu_ref, bglu_ref, wb_ref, wout_ref,
                 wfg_ref, wfu_ref, wfd_ref, wpg_ref, wpp_ref, o_ref, *, ff_chunks, final_norm):
    x = x_ref[...]
    h = _rms(x, gmix_ref[...]).astype(MXU_DTYPE)
    branch_a = _mxu(attn_ref[...], wa_ref)
    y = jax.nn.gelu(ypre_ref[...])
    y = y * jax.nn.sigmoid(_mxu(y, wglu_ref) + bglu_ref[...])
    branch_b = _mxu(y, wb_ref)
    merged = (jax.nn.sigmoid(jnp.dot(h, wga_ref[...], preferred_element_type=F32)) * branch_a
              + jax.nn.sigmoid(jnp.dot(h, wgb_ref[...], preferred_element_type=F32)) * branch_b)
    x = x + _mxu(merged, wout_ref)
    h = _rms(x, gffn_ref[...]).astype(MXU_DTYPE)
    d_ff = wfg_ref.shape[1]
    fc = d_ff // ff_chunks
    ffn = jnp.zeros_like(x)
    for c in range(ff_chunks):
        gate = jnp.dot(h, wfg_ref[:, c * fc:(c + 1) * fc], preferred_element_type=F32)
        up = jnp.dot(h, wfu_ref[:, c * fc:(c + 1) * fc], preferred_element_type=F32)
        act = (jax.nn.silu(gate) * up).astype(MXU_DTYPE)
        ffn = ffn + jnp.dot(act, wfd_ref[c * fc:(c + 1) * fc, :], preferred_element_type=F32)
    x = x + ffn
    h = _rms(x, gple_ref[...])
    x = x + jax.nn.sigmoid(_mxu(h, wpg_ref)) * _mxu(p_ref[...], wpp_ref)
    o_ref[...] = _rms(x, gfin_ref[...]) if final_norm else x


def _tail(x, attn, ypre, p, gains, weights, tm, final_norm, ff_chunks=2):
    rows, d = x.shape
    assert rows % tm == 0
    row = lambda a: pl.BlockSpec((tm, a.shape[1]), lambda i: (i, 0))
    cst = lambda a: pl.BlockSpec(a.shape, lambda i: (0, 0), pipeline_mode=pl.Buffered(1))
    return pl.pallas_call(
        functools.partial(_tail_kernel, ff_chunks=ff_chunks, final_norm=final_norm),
        grid=(rows // tm,),
        in_specs=[row(x), row(attn), row(ypre), row(p)] + [cst(g) for g in gains] + [cst(w) for w in weights],
        out_specs=row(x),
        out_shape=jax.ShapeDtypeStruct(x.shape, F32),
        compiler_params=_cparams(("arbitrary",)),
        name="merge_ffn_ple",
    )(x, attn, ypre, p, *gains, *weights)


SCORE_CHUNK = 1024
KV_RING = 32
PAGE_UNROLL = 8


def _sample_attn_kernel(pt_ref, q_ref, qi_ref, kn_ref, vn_ref, kiwi_ref, kidx_hbm, ck_hbm, cv_hbm, o_ref,
                        kib_ref, kv_ref, sc_ref, sel_ref, lg_ref, acc_ref, tie_ref, kisem, kvsem,
                        *, group, t_new, n_pages, n_sel, n_steps):
    step = pl.program_id(0)
    past = n_pages * PAGE_SIZE
    width = past + LANES
    rows = group * t_new
    hq = N_HEADS * t_new
    chunks_per_seq = 2 * n_pages
    total_chunks = n_steps * group * chunks_per_seq
    lookahead = KV_RING - 1

    def ki_copy(s, g, page):
        src = kidx_hbm.at[pt_ref[(s * group + g) * n_pages + page]]
        dst = kib_ref.at[s % 2, g, :, pl.ds(pl.multiple_of(page * PAGE_SIZE, PAGE_SIZE), PAGE_SIZE)]
        return pltpu.make_async_copy(src, dst, kisem.at[s % 2])

    def for_ki_pages(s, fn):
        for g in range(group):
            def body(page, c, g=g):
                fn(ki_copy(s, g, page))
                return c
            lax.fori_loop(0, n_pages, body, 0)

    def kv_copy(gc, src_hbm):
        seq = gc // chunks_per_seq
        page = gc % n_pages
        return pltpu.make_async_copy(src_hbm.at[pt_ref[seq * n_pages + page]], kv_ref.at[gc % KV_RING],
                                     kvsem.at[gc % KV_RING])

    def issue_kv(gc):
        is_k = (gc % chunks_per_seq) < n_pages

        @pl.when((gc < total_chunks) & is_k)
        def _():
            kv_copy(gc, ck_hbm).start()

        @pl.when((gc < total_chunks) & jnp.logical_not(is_k))
        def _():
            kv_copy(gc, cv_hbm).start()

    @pl.when(step == 0)
    def _():
        for_ki_pages(step, lambda cp: cp.start())
        for c in range(lookahead):
            issue_kv(jnp.int32(c))

    for_ki_pages(step, lambda cp: cp.wait())

    @pl.when(step + 1 < n_steps)
    def _():
        for_ki_pages(step + 1, lambda cp: cp.start())

    lane1 = lax.broadcasted_iota(I32, (t_new, LANES), 1)
    row1 = lax.broadcasted_iota(I32, (t_new, LANES), 0)
    new_visible = lane1 <= row1

    def pad_rows(a):
        return jnp.concatenate([a, jnp.zeros((LANES - t_new, a.shape[1]), a.dtype)], axis=0)

    qi_hq, w_col = [], []
    for g in range(group):
        qi_g = qi_ref[g * t_new:(g + 1) * t_new, :].astype(MXU_DTYPE)
        qi_hq.append(jnp.concatenate([qi_g[:, h * IDX_DIM:(h + 1) * IDX_DIM] for h in range(IDX_HEADS)], axis=0))
        w_col.append([kiwi_ref[g * t_new:(g + 1) * t_new, IDX_DIM + h:IDX_DIM + h + 1] for h in range(IDX_HEADS)])

    def head_sum(s, w):
        acc = jnp.zeros((t_new, s.shape[1]), F32)
        for h in range(IDX_HEADS):
            acc = acc + jnp.maximum(s[h * t_new:(h + 1) * t_new, :], 0.0) * w[h]
        return acc

    def score_chunk(c, carry):
        off = pl.multiple_of(c * SCORE_CHUNK, SCORE_CHUNK)
        for g in range(group):
            kc = kib_ref[step % 2, g, :, pl.ds(off, SCORE_CHUNK)].astype(MXU_DTYPE)
            s = jnp.dot(qi_hq[g], kc, preferred_element_type=F32)
            sc_ref[g * t_new:(g + 1) * t_new, pl.ds(off, SCORE_CHUNK)] = head_sum(s, w_col[g])
        return carry

    lax.fori_loop(0, past // SCORE_CHUNK, score_chunk, 0)
    for g in range(group):
        ki_new = pad_rows(kiwi_ref[g * t_new:(g + 1) * t_new, :IDX_DIM]).astype(MXU_DTYPE)
        s_new = jnp.where(new_visible, head_sum(_nt_dot(qi_hq[g], ki_new), w_col[g]), NEG)
        sc_ref[g * t_new:(g + 1) * t_new, past:width] = jnp.where(lane1 < t_new, s_new, -jnp.inf)

    n_tiles = width // LANES
    n_acc = 4

    def count(pred):
        accs = [jnp.zeros((rows, LANES), I32)] * n_acc
        for j in range(n_tiles):
            accs[j % n_acc] = accs[j % n_acc] + pred(sc_ref[:, j * LANES:(j + 1) * LANES], j * LANES).astype(I32)
        return jnp.sum((accs[0] + accs[1]) + (accs[2] + accs[3]), axis=1, keepdims=True)

    thr = _kth_largest(lambda cand, key: count(lambda kk, off: kk >= cand), n_sel, (rows, 1))
    cnt_ge = count(lambda kk, off: kk >= thr)
    need = n_sel - count(lambda kk, off: kk > thr)

    idx_bits = (width - 1).bit_length()
    lane_r = lax.broadcasted_iota(I32, (rows, LANES), 1)
    tie_ref[...] = jnp.full((rows, 1), (1 << idx_bits) - 1, I32)

    @pl.when(jnp.max(cnt_ge) > n_sel)
    def _():
        def idx_step(i, x):
            cand = x | jnp.left_shift(jnp.int32(1), idx_bits - 1 - i)
            c = count(lambda kk, off: (kk == thr) & ((lane_r + off) < cand))
            return jnp.where(c < need, cand, x)
        tie_ref[...] = lax.fori_loop(0, idx_bits, idx_step, jnp.zeros((rows, 1), I32))

    tie_x = tie_ref[...]
    new_vis_rows = jnp.concatenate([new_visible] * group, axis=0)
    for j in range(n_tiles):
        kk = sc_ref[:, j * LANES:(j + 1) * LANES]
        taken = jnp.where(kk > thr, 1, jnp.where(kk == thr, jnp.where((lane_r + j * LANES) <= tie_x, 1, 0), 0))
        if j == n_tiles - 1:
            taken = jnp.where(new_vis_rows, taken, 0)
        sel_ref[:, j * LANES:(j + 1) * LANES] = taken.astype(F32)

    head_of_row = lax.broadcasted_iota(I32, (hq, ATTN_WIDTH), 0) // t_new
    head_of_lane = lax.broadcasted_iota(I32, (hq, ATTN_WIDTH), 1) // HEAD_DIM
    for g in range(group):
        r0 = g * t_new
        chunk0 = (step * group + g) * chunks_per_seq
        q_g = q_ref[r0:r0 + t_new, :]
        q_bd = jnp.where(head_of_row == head_of_lane, jnp.concatenate([q_g] * N_HEADS, axis=0), 0.0).astype(MXU_DTYPE)

        def sel_hq(lo, r0=r0):
            s = sel_ref[r0:r0 + t_new, pl.ds(lo, LANES)]
            return jnp.concatenate([s] * N_HEADS, axis=0) > 0.0

        def k_pages(it, m, chunk0=chunk0, q_bd=q_bd, sel_hq=sel_hq):
            gc0 = chunk0 + it * PAGE_UNROLL
            for un in range(PAGE_UNROLL):
                kv_copy(gc0 + un, ck_hbm).wait()
            for un in range(PAGE_UNROLL):
                kp = kv_ref[(gc0 + un) % KV_RING].astype(MXU_DTYPE)
                off = pl.multiple_of((it * PAGE_UNROLL + un) * PAGE_SIZE, PAGE_SIZE)
                lg = jnp.where(sel_hq(off), jnp.dot(q_bd, kp, preferred_element_type=F32), NEG)
                lg_ref[:, pl.ds(off, LANES)] = lg
                m = jnp.maximum(m, lg)
            for un in range(PAGE_UNROLL):
                issue_kv(gc0 + un + lookahead)
            return m

        m = lax.fori_loop(0, n_pages // PAGE_UNROLL, k_pages, jnp.full((hq, LANES), -jnp.inf, F32))
        k_new = pad_rows(kn_ref[r0:r0 + t_new, :]).astype(MXU_DTYPE)
        lg_new = jnp.where(sel_hq(past), _nt_dot(q_bd, k_new), NEG)
        lg_ref[:, past:width] = lg_new
        m = jnp.max(jnp.maximum(m, lg_new), axis=1, keepdims=True)

        acc_ref[...] = jnp.zeros_like(acc_ref)

        def v_pages(it, den, chunk0=chunk0, m=m):
            gc0 = chunk0 + n_pages + it * PAGE_UNROLL
            for un in range(PAGE_UNROLL):
                kv_copy(gc0 + un, ck_hbm).wait()
            acc = acc_ref[...]
            for un in range(PAGE_UNROLL):
                vp = kv_ref[(gc0 + un) % KV_RING].astype(MXU_DTYPE)
                off = pl.multiple_of((it * PAGE_UNROLL + un) * PAGE_SIZE, PAGE_SIZE)
                e = jnp.exp(lg_ref[:, pl.ds(off, LANES)] - m)
                acc = acc + _nt_dot(e.astype(MXU_DTYPE), vp)
                den = den + e
            acc_ref[...] = acc
            for un in range(PAGE_UNROLL):
                issue_kv(gc0 + un + lookahead)
            return den

        den = lax.fori_loop(0, n_pages // PAGE_UNROLL, v_pages, jnp.zeros((hq, LANES), F32))
        e_new = jnp.exp(lg_ref[:, past:width] - m)
        v_new = pad_rows(vn_ref[r0:r0 + t_new, :]).astype(MXU_DTYPE)
        acc = acc_ref[...] + jnp.dot(e_new.astype(MXU_DTYPE), v_new, preferred_element_type=F32)
        o = acc / jnp.sum(den + e_new, axis=1, keepdims=True)
        o = jnp.where(head_of_row == head_of_lane, o, 0.0)
        out = o[0:t_new, :]
        for h in range(1, N_HEADS):
            out = out + o[h * t_new:(h + 1) * t_new, :]
        o_ref[r0:r0 + t_new, :] = out


def _sample_attention(q, qi, k_new, v_new, kiwi, kidx_pages, k_pages, v_pages, page_table, t_new, group):
    rows_tot = q.shape[0]
    n_seq, n_pages = page_table.shape
    assert n_seq % group == 0 and (n_pages * PAGE_SIZE) % SCORE_CHUNK == 0 and t_new <= SUBLANES
    assert n_pages % PAGE_UNROLL == 0
    n_steps = n_seq // group
    rows = group * t_new
    past = n_pages * PAGE_SIZE
    width = past + LANES
    n_sel = min(TOPK_MAX, (past + t_new) // 4)
    row = lambda w: pl.BlockSpec((rows, w), lambda i, pt: (i, 0))
    hbm = pl.BlockSpec(memory_space=pl.ANY)
    grid_spec = pltpu.PrefetchScalarGridSpec(
        num_scalar_prefetch=1,
        grid=(n_steps,),
        in_specs=[row(ATTN_WIDTH), row(ATTN_WIDTH), row(ATTN_WIDTH), row(ATTN_WIDTH), row(LANES), hbm, hbm, hbm],
        out_specs=row(ATTN_WIDTH),
        scratch_shapes=[pltpu.VMEM((2, group, IDX_DIM, past), F32),
                        pltpu.VMEM((KV_RING, ATTN_WIDTH, PAGE_SIZE), F32),
                        pltpu.VMEM((rows, width), F32),
                        pltpu.VMEM((rows, width), F32),
                        pltpu.VMEM((N_HEADS * t_new, width), F32),
                        pltpu.VMEM((N_HEADS * t_new, ATTN_WIDTH), F32),
                        pltpu.VMEM((rows, 1), I32),
                        pltpu.SemaphoreType.DMA((2,)),
                        pltpu.SemaphoreType.DMA((KV_RING,))])
    return pl.pallas_call(
        functools.partial(_sample_attn_kernel, group=group, t_new=t_new, n_pages=n_pages, n_sel=n_sel,
                          n_steps=n_steps),
        grid_spec=grid_spec,
        out_shape=jax.ShapeDtypeStruct((rows_tot, ATTN_WIDTH), F32),
        compiler_params=_cparams(("arbitrary",)),
        name="sample_attention",
    )(page_table.reshape(-1), q, qi, k_new, v_new, kiwi, kidx_pages, k_pages, v_pages)


IN_SIZES = (ATTN_WIDTH, ATTN_WIDTH, ATTN_WIDTH, IDX_HEADS * IDX_DIM, IDX_DIM, IDX_HEADS, SSM_WIDTH)
PROMPT_ROW_TILE = 512
SAMPLE_GROUP = 4
SSM_TIME_CHUNK = 128
SSM_PASSES = 2


def _split_w_in(w):
    d = w.shape[0]
    offs = np.cumsum((0,) + IN_SIZES)
    o_ki, o_u, o_ga = int(offs[4]), int(offs[6]), int(offs[7])
    kiwi = jnp.pad(w[:, o_ki:o_u], ((0, 0), (0, LANES - (o_u - o_ki))))
    w_main = jnp.concatenate([w[:, :o_ki], kiwi, w[:, o_u:o_ga]], axis=1).astype(MXU_DTYPE)
    return w_main, w[:, o_ga:o_ga + d].astype(MXU_DTYPE), w[:, o_ga + d:o_ga + 2 * d].astype(MXU_DTYPE)


def kernel(x_prompt, x_sample, p_prompt, p_sample, cache_k, cache_v, cache_kidx, state_ssm_re, state_ssm_im, page_table, norm_mix, w_in, w_branch_a, w_branch_b, w_out, ssm_a_re, ssm_a_im, ssm_log_dt, ssm_b_re, ssm_b_im, ssm_c_re, ssm_c_im, ssm_d, w_glu, b_glu, norm_ffn, w_ff_gate, w_ff_up, w_ff_down, norm_ple, w_ple_gate, w_ple_proj, norm_final):
    b, s, d = x_prompt.shape
    db, t, _ = x_sample.shape
    depth = w_in.shape[0]
    n_pages = page_table.shape[1]
    past = n_pages * PAGE_SIZE
    n_phys = cache_k.shape[1]
    rows_p, rows_s = b * s, db * t
    tm_p = min(PROMPT_ROW_TILE, s)
    tm_s = min(PROMPT_ROW_TILE, rows_s)
    tables_p = _rope_tables(s, 0, s)
    tables_s = _rope_tables(tm_s, past, t)
    bf = lambda a: a.astype(MXU_DTYPE)
    row = lambda a: a.reshape(1, -1)

    xp = x_prompt.reshape(rows_p, d)
    xs = x_sample.reshape(rows_s, d)
    st_p, st_s = [], []
    for i in range(depth):
        w_main, w_ga, w_gb = _split_w_in(w_in[i])
        abr, abi, bbr, bbi = _ssm_discretize(ssm_a_re[i], ssm_a_im[i], ssm_log_dt[i], ssm_b_re[i], ssm_b_im[i])
        blocks = _ssm_block_weights(bbr, bbi, ssm_c_re[i], ssm_c_im[i])
        gains = [row(norm_mix[i]), row(norm_ffn[i]), row(norm_ple[i]), row(norm_final)]
        weights = [w_ga, w_gb, bf(w_branch_a[i]), bf(w_glu[i]), row(b_glu[i]), bf(w_branch_b[i]), bf(w_out[i]),
                   bf(w_ff_gate[i]), bf(w_ff_up[i]), bf(w_ff_down[i]), bf(w_ple_gate[i]), bf(w_ple_proj[i])]
        final = i == depth - 1

        (q, kbf, kt, _, vt, vtbf, qi, _, kiwit, kidup, u) = _inproj(
            xp.reshape(b, s, d), gains[0], w_main, tables_p, tm_p)
        attn = _prompt_attention(q, qi, kiwit, kbf, kidup, vtbf, b, s)
        zeros = jnp.zeros((b, SSM_COLS), F32)
        ypre, hr, hi = _ssm_scan(u.reshape(b, s, SSM_WIDTH), zeros, zeros, row(abr), row(abi), blocks,
                                 row(ssm_d[i]), min(SSM_TIME_CHUNK, s), SSM_PASSES)
        xp = _tail(xp, attn, ypre.reshape(rows_p, SSM_WIDTH), p_prompt[i].reshape(rows_p, -1), gains, weights,
                   tm_p, final)
        heads = lambda a: a.reshape(b, N_HEADS, HEAD_DIM, s).transpose(0, 3, 1, 2)
        st_p.append((heads(kt), heads(vt), kiwit[:, :IDX_DIM, :].transpose(0, 2, 1),
                     hr.reshape(b, SSM_GROUPS, SSM_STATE), hi.reshape(b, SSM_GROUPS, SSM_STATE)))

        (q, kbf, kt, vbf, vt, _, qi, kiwi, _, _, u) = _inproj(
            xs.reshape(1, rows_s, d), gains[0], w_main, tables_s, tm_s)
        f32 = lambda a: a.astype(F32)
        attn = _sample_attention(
            f32(q), f32(qi), f32(kbf), f32(vbf), kiwi,
            cache_kidx[i].transpose(0, 2, 1),
            cache_k[i].transpose(0, 2, 3, 1).reshape(n_phys, ATTN_WIDTH, PAGE_SIZE),
            cache_v[i].transpose(0, 2, 3, 1).reshape(n_phys, ATTN_WIDTH, PAGE_SIZE),
            page_table, t, SAMPLE_GROUP)
        ypre, hr, hi = _ssm_scan(u.reshape(db, t, SSM_WIDTH), state_ssm_re[i].reshape(db, SSM_COLS),
                                 state_ssm_im[i].reshape(db, SSM_COLS), row(abr), row(abi), blocks,
                                 row(ssm_d[i]), t, SSM_PASSES)
        xs = _tail(xs, attn, ypre.reshape(rows_s, SSM_WIDTH), p_sample[i].reshape(rows_s, -1), gains, weights,
                   tm_s, final)
        heads = lambda a: a.reshape(N_HEADS, HEAD_DIM, db, t).transpose(2, 3, 0, 1)
        st_s.append((heads(kt), heads(vt), kiwi[:, :IDX_DIM].reshape(db, t, IDX_DIM),
                     hr.reshape(db, SSM_GROUPS, SSM_STATE), hi.reshape(db, SSM_GROUPS, SSM_STATE)))

    stack = lambda sts, j: jnp.stack([st[j] for st in sts])
    return (xp.reshape(b, s, d), xs.reshape(db, t, d),
            *[stack(st_p, j) for j in range(5)], *[stack(st_s, j) for j in range(5)])
```

```python
import functools
import math

import numpy as np
import jax
import jax.numpy as jnp
from jax import lax
from jax.experimental import pallas as pl
from jax.experimental.pallas import tpu as pltpu

N_HEADS = 8
HEAD_DIM = 64
ATTN_WIDTH = N_HEADS * HEAD_DIM
IDX_HEADS = 8
IDX_DIM = 64
TOPK_MAX = 256
ROPE_THETA = 500000.0
ROPE_ROT = HEAD_DIM // 4
ROPE_HALF = ROPE_ROT // 2
SSM_GROUP = 16
SSM_GROUPS = 32
SSM_WIDTH = SSM_GROUP * SSM_GROUPS
SSM_STATE = 64
SSM_COLS = SSM_GROUPS * SSM_STATE
PAGE_SIZE = 128
Q_BLOCK = 128
EPS = 1e-6
NEG = -1e30

LANES = 128
SUBLANES = 8
VMEM_LIMIT = 56 * 1024 * 1024

MXU_DTYPE = jnp.bfloat16
F32 = jnp.float32
I32 = jnp.int32
INT_MIN = -(2 ** 31)


def _key_to_f32(key):
    return pltpu.bitcast(key ^ ((key >> 31) & 0x7FFFFFFF), F32)


def _kth_largest(count_ge, n_sel, shape):
    zero = jnp.zeros(shape, I32)
    base = jnp.where(count_ge(_key_to_f32(zero), zero) >= n_sel, 0, INT_MIN).astype(I32)

    def bit_step(i, base):
        cand = base | jnp.left_shift(jnp.int32(1), 30 - i)
        return jnp.where(count_ge(_key_to_f32(cand), cand) >= n_sel, cand, base)

    return _key_to_f32(lax.fori_loop(0, 31, bit_step, base))


def _cparams(sem):
    return pltpu.CompilerParams(dimension_semantics=sem, vmem_limit_bytes=VMEM_LIMIT)


def _nt_dot(a, b):
    return lax.dot_general(a, b, (((1,), (1,)), ((), ())), preferred_element_type=F32)


def _rope_table_kernel(invf_ref, c_ref, s1_ref, s2_ref, *, pos0, period, rows):
    i = pl.program_id(0)
    r = lax.broadcasted_iota(I32, (rows, LANES), 0) + i * rows
    pos = (pos0 + r % period).astype(F32)
    ang = pos * invf_ref[...]
    lane = lax.broadcasted_iota(I32, (rows, LANES), 1) % HEAD_DIM
    c = jnp.cos(ang)
    s = jnp.sin(ang)
    c_ref[...] = jnp.where(lane < ROPE_ROT, c, 1.0)
    s1_ref[...] = jnp.where(lane < ROPE_HALF, -s, 0.0)
    s2_ref[...] = jnp.where((lane >= ROPE_HALF) & (lane < ROPE_ROT), s, 0.0)


def _rope_tables(n_rows, pos0, period):
    inv_freq = ROPE_THETA ** (-jnp.arange(0, ROPE_ROT, 2, dtype=F32) / ROPE_ROT)
    lane = np.arange(LANES) % HEAD_DIM
    invf = jnp.where(lane < ROPE_ROT, inv_freq[lane % ROPE_HALF], 0.0).astype(F32)[None, :]
    rows = min(n_rows, 512)
    assert n_rows % rows == 0
    spec = pl.BlockSpec((rows, LANES), lambda i: (i, 0))
    return pl.pallas_call(
        functools.partial(_rope_table_kernel, pos0=pos0, period=period, rows=rows),
        grid=(n_rows // rows,),
        in_specs=[pl.BlockSpec((1, LANES), lambda i: (0, 0))],
        out_specs=[spec, spec, spec],
        out_shape=[jax.ShapeDtypeStruct((n_rows, LANES), F32)] * 3,
        compiler_params=_cparams(("arbitrary",)),
        name="rope_tables",
    )(invf)


W_Q, W_K, W_V, W_QI, W_KIWI, W_U = 0, 512, 1024, 1536, 2048, 2176
W_MAIN = 2688


def _inproj_kernel(x_ref, g_ref, w_ref, c_ref, s1_ref, s2_ref,
                   q_ref, kbf_ref, kt_ref, vbf_ref, vt_ref, vtbf_ref, qi_ref,
                   kiwi_ref, kiwit_ref, kidup_ref, u_ref):
    x = x_ref[...]
    ms = jnp.mean(x * x, axis=-1, keepdims=True)
    h = (x * lax.rsqrt(ms + EPS) * g_ref[...]).astype(MXU_DTYPE)
    cos_t, sin_lo, sin_hi = c_ref[...], s1_ref[...], s2_ref[...]

    def proj(lo, hi):
        return jnp.dot(h, w_ref[:, lo:hi], preferred_element_type=F32)

    def rope_tile(t):
        return t * cos_t + pltpu.roll(t, LANES - ROPE_HALF, 1) * sin_lo + pltpu.roll(t, ROPE_HALF, 1) * sin_hi

    def rope(z):
        return jnp.concatenate([rope_tile(z[:, j * LANES:(j + 1) * LANES]) for j in range(z.shape[1] // LANES)], axis=1)

    q_ref[...] = (rope(proj(W_Q, W_K)) * (HEAD_DIM ** -0.5)).astype(q_ref.dtype)
    k = rope(proj(W_K, W_V))
    kbf_ref[...] = k.astype(kbf_ref.dtype)
    kt_ref[...] = k.T
    v = proj(W_V, W_QI)
    vbf_ref[...] = v.astype(vbf_ref.dtype)
    vt = v.T
    vt_ref[...] = vt
    vtbf_ref[...] = vt.astype(vtbf_ref.dtype)
    qi_ref[...] = (rope(proj(W_QI, W_KIWI)) * (IDX_DIM ** -0.5)).astype(qi_ref.dtype)
    z = proj(W_KIWI, W_U)
    lane = lax.broadcasted_iota(I32, z.shape, 1)
    kiwi = jnp.where(lane < IDX_DIM, rope_tile(z), z * (IDX_HEADS ** -0.5))
    kiwi_ref[...] = kiwi
    kiwit_ref[...] = kiwi.T
    kidup_ref[...] = jnp.where(lane < IDX_DIM, kiwi, pltpu.roll(kiwi, IDX_DIM, 1)).astype(kidup_ref.dtype)
    u_ref[...] = proj(W_U, W_MAIN)


def _inproj(x, gain, w_main, tables, tm):
    bk, sk, d = x.shape
    nt = sk // tm
    assert sk % tm == 0
    cos_t, sin_lo, sin_hi = tables
    n_tab = cos_t.shape[0] // tm
    rows = bk * sk
    xf = x.reshape(rows, d)
    row = lambda w: pl.BlockSpec((tm, w), lambda i: (i, 0))
    tr = lambda w: pl.BlockSpec((None, w, tm), lambda i: (i // nt, 0, i % nt))
    tab = pl.BlockSpec((tm, LANES), lambda i: (i % n_tab, 0))
    outs = [
        (row(ATTN_WIDTH), (rows, ATTN_WIDTH), MXU_DTYPE),
        (row(ATTN_WIDTH), (rows, ATTN_WIDTH), MXU_DTYPE),
        (tr(ATTN_WIDTH), (bk, ATTN_WIDTH, sk), F32),
        (row(ATTN_WIDTH), (rows, ATTN_WIDTH), MXU_DTYPE),
        (tr(ATTN_WIDTH), (bk, ATTN_WIDTH, sk), F32),
        (tr(ATTN_WIDTH), (bk, ATTN_WIDTH, sk), MXU_DTYPE),
        (row(ATTN_WIDTH), (rows, ATTN_WIDTH), MXU_DTYPE),
        (row(LANES), (rows, LANES), F32),
        (tr(LANES), (bk, LANES, sk), F32),
        (row(LANES), (rows, LANES), MXU_DTYPE),
        (row(SSM_WIDTH), (rows, SSM_WIDTH), F32),
    ]
    return pl.pallas_call(
        _inproj_kernel,
        grid=(rows // tm,),
        in_specs=[row(d),
                  pl.BlockSpec((1, d), lambda i: (0, 0)),
                  pl.BlockSpec((d, W_MAIN), lambda i: (0, 0)),
                  tab, tab, tab],
        out_specs=[o[0] for o in outs],
        out_shape=[jax.ShapeDtypeStruct(o[1], o[2]) for o in outs],
        compiler_params=_cparams(("arbitrary",)),
        name="inproj",
    )(xf, gain, w_main, cos_t, sin_lo, sin_hi)


KEY_CHUNK = 512


def _head_masked_pair(tile, lane_lo):
    zero = jnp.zeros_like(tile)
    return jnp.concatenate([jnp.where(lane_lo, tile, zero), jnp.where(lane_lo, zero, tile)], axis=0)


def _prompt_attn_kernel(q_ref, qi_ref, kw_ref, k_ref, kidup_ref, vt_ref, o_ref,
                        sc_ref, lg_ref, ot_ref, rhs_ref, wi_ref, tie_ref, *, seq, n_sel):
    qb = pl.program_id(1)
    lane_lo = lax.broadcasted_iota(I32, (Q_BLOCK, LANES), 1) < HEAD_DIM
    lane_q = lax.broadcasted_iota(I32, (1, Q_BLOCK), 1)
    row_iota = lax.broadcasted_iota(I32, (KEY_CHUNK, Q_BLOCK), 0)
    n_pairs = N_HEADS // 2
    groups = KEY_CHUNK // SUBLANES

    def n_chunks(blk):
        return ((blk + 1) * Q_BLOCK + KEY_CHUNK - 1) // KEY_CHUNK

    nch = n_chunks(qb)
    n_virtual = seq - nch * KEY_CHUNK
    qpos = qb * Q_BLOCK + lane_q

    def chunk_rows(c):
        return pl.ds(pl.multiple_of(c * KEY_CHUNK, KEY_CHUNK), KEY_CHUNK)

    def group_reduce(op, a):
        return op(a.reshape(groups, SUBLANES, a.shape[1]), axis=0)

    for p in range(n_pairs):
        rhs_ref[p] = _head_masked_pair(qi_ref[:, p * LANES:(p + 1) * LANES], lane_lo)
        rhs_ref[n_pairs + p] = _head_masked_pair(q_ref[:, p * LANES:(p + 1) * LANES], lane_lo)
    wi_ref[...] = kw_ref[IDX_DIM:IDX_DIM + IDX_HEADS, :]

    def produce_chunk(c, carry):
        rows = chunk_rows(c)
        kd = kidup_ref[rows, :]
        acc = jnp.zeros((KEY_CHUNK, Q_BLOCK), F32)
        for p in range(n_pairs):
            s = _nt_dot(kd, rhs_ref[p])
            acc = acc + jnp.maximum(s[:, :Q_BLOCK], 0.0) * wi_ref[2 * p:2 * p + 1, :]
            acc = acc + jnp.maximum(s[:, Q_BLOCK:], 0.0) * wi_ref[2 * p + 1:2 * p + 2, :]
        visible = (row_iota + c * KEY_CHUNK) <= qpos
        sc_ref[rows, :] = jnp.where(visible, acc, NEG)
        for p in range(n_pairs):
            lg = _nt_dot(k_ref[rows, p * LANES:(p + 1) * LANES], rhs_ref[n_pairs + p])
            lg_ref[p, rows, :] = lg.astype(lg_ref.dtype)
        return carry

    lax.fori_loop(0, nch, produce_chunk, 0)

    def count(pred):
        def body(c, acc):
            m = pred(sc_ref[chunk_rows(c), :], c * KEY_CHUNK).astype(I32)
            return acc + group_reduce(jnp.sum, m)
        acc = lax.fori_loop(0, nch, body, jnp.zeros((SUBLANES, Q_BLOCK), I32))
        return jnp.sum(acc, axis=0, keepdims=True)

    def count_ge(cand, key):
        return count(lambda sc, off: sc >= cand) + jnp.where(cand <= NEG, n_virtual, 0)

    thr = _kth_largest(count_ge, n_sel, (1, Q_BLOCK))
    cnt_ge = count_ge(thr, None)
    cnt_gt = count(lambda sc, off: sc > thr) + jnp.where(thr < NEG, n_virtual, 0)
    need = n_sel - cnt_gt

    idx_bits = max(1, (seq - 1).bit_length())
    tie_ref[...] = jnp.full((1, Q_BLOCK), (1 << idx_bits) - 1, I32)

    @pl.when(jnp.max(cnt_ge) > n_sel)
    def _():
        def idx_step(i, x):
            cand = x | jnp.left_shift(jnp.int32(1), idx_bits - 1 - i)
            c = count(lambda kk, off: (kk == thr) & ((row_iota + off) < cand))
            return jnp.where(c < need, cand, x)
        tie_ref[...] = lax.fori_loop(0, idx_bits, idx_step, jnp.zeros((1, Q_BLOCK), I32))

    tie_x = tie_ref[...]

    def mask_chunk(c, ms):
        rows = chunk_rows(c)
        sc = sc_ref[rows, :]
        idx = row_iota + c * KEY_CHUNK
        taken = jnp.where(sc > thr, 1, jnp.where(sc == thr, jnp.where(idx <= tie_x, 1, 0), 0))
        sel = jnp.where(idx <= qpos, taken, 0) > 0
        out = []
        for p in range(n_pairs):
            lo = jnp.where(sel, lg_ref[p, rows, :Q_BLOCK], NEG)
            hi = jnp.where(sel, lg_ref[p, rows, Q_BLOCK:], NEG)
            lg_ref[p, rows, :Q_BLOCK] = lo
            lg_ref[p, rows, Q_BLOCK:] = hi
            out.append(jnp.maximum(ms[2 * p], group_reduce(jnp.max, lo)))
            out.append(jnp.maximum(ms[2 * p + 1], group_reduce(jnp.max, hi)))
        return tuple(out)

    ms = lax.fori_loop(0, nch, mask_chunk, (jnp.full((SUBLANES, Q_BLOCK), -jnp.inf, F32),) * N_HEADS)
    ms = [jnp.max(m, axis=0, keepdims=True) for m in ms]

    ot_ref[...] = jnp.zeros_like(ot_ref)

    def pv_chunk(c, dens):
        rows = chunk_rows(c)
        out = []
        for h in range(N_HEADS):
            half = slice((h % 2) * Q_BLOCK, (h % 2 + 1) * Q_BLOCK)
            e = jnp.exp(lg_ref[h // 2, rows, half] - ms[h])
            out.append(dens[h] + group_reduce(jnp.sum, e))
            v = vt_ref[h * HEAD_DIM:(h + 1) * HEAD_DIM, rows]
            ot_ref[h * HEAD_DIM:(h + 1) * HEAD_DIM, :] += jnp.dot(v, e.astype(MXU_DTYPE), preferred_element_type=F32)
        return tuple(out)

    dens = lax.fori_loop(0, nch, pv_chunk, (jnp.zeros((SUBLANES, Q_BLOCK), F32),) * N_HEADS)
    for h in range(N_HEADS):
        den = jnp.sum(dens[h], axis=0, keepdims=True)
        ot_ref[h * HEAD_DIM:(h + 1) * HEAD_DIM, :] = ot_ref[h * HEAD_DIM:(h + 1) * HEAD_DIM, :] / den
    o_ref[...] = ot_ref[...].T.astype(o_ref.dtype)


def _prompt_attention(q, qi, kiwit, k, kidup, vt, batch, seq):
    nqb = seq // Q_BLOCK
    assert seq % KEY_CHUNK == 0
    n_sel = min(TOPK_MAX, seq // 4)
    qspec = pl.BlockSpec((Q_BLOCK, ATTN_WIDTH), lambda b, j: (b * nqb + j, 0))
    return pl.pallas_call(
        functools.partial(_prompt_attn_kernel, seq=seq, n_sel=n_sel),
        grid=(batch, nqb),
        in_specs=[qspec, qspec,
                  pl.BlockSpec((None, LANES, Q_BLOCK), lambda b, j: (b, 0, j)),
                  pl.BlockSpec((seq, ATTN_WIDTH), lambda b, j: (b, 0)),
                  pl.BlockSpec((seq, LANES), lambda b, j: (b, 0)),
                  pl.BlockSpec((None, ATTN_WIDTH, seq), lambda b, j: (b, 0, 0))],
        out_specs=qspec,
        out_shape=jax.ShapeDtypeStruct((batch * seq, ATTN_WIDTH), MXU_DTYPE),
        scratch_shapes=[pltpu.VMEM((seq, Q_BLOCK), F32),
                        pltpu.VMEM((N_HEADS // 2, seq, 2 * Q_BLOCK), F32),
                        pltpu.VMEM((ATTN_WIDTH, Q_BLOCK), F32),
                        pltpu.VMEM((N_HEADS, 2 * Q_BLOCK, LANES), MXU_DTYPE),
                        pltpu.VMEM((IDX_HEADS, Q_BLOCK), F32),
                        pltpu.VMEM((1, Q_BLOCK), I32)],
        compiler_params=_cparams(("arbitrary", "arbitrary")),
        name="prompt_attention",
    )(q, qi, kiwit, k, kidup, vt)


def _ssm_disc_kernel(are_ref, aim_ref, ldt_ref, bre_ref, bim_ref, abr_ref, abi_ref, bbr_ref, bbi_ref):
    are, aim = are_ref[...], aim_ref[...]
    dt = jnp.exp(ldt_ref[...])
    mag = jnp.exp(are * dt)
    abr = mag * jnp.cos(aim * dt)
    abi = mag * jnp.sin(aim * dt)
    abr_ref[...] = abr
    abi_ref[...] = abi
    nr, ni = abr - 1.0, abi
    den = are * are + aim * aim
    fr = (nr * are + ni * aim) / den
    fi = (ni * are - nr * aim) / den
    bre, bim = bre_ref[...], bim_ref[...]
    bbr_ref[...] = fr * bre - fi * bim
    bbi_ref[...] = fr * bim + fi * bre


def _ssm_discretize(a_re, a_im, log_dt, b_re, b_im):
    col = lambda a: a.reshape(SSM_COLS, 1)
    ldt = jnp.broadcast_to(log_dt[:, None], (SSM_GROUPS, SSM_STATE))
    flat = lambda b: b.reshape(SSM_COLS, SSM_GROUP)
    shp1 = jax.ShapeDtypeStruct((SSM_COLS, 1), F32)
    shpb = jax.ShapeDtypeStruct((SSM_COLS, SSM_GROUP), F32)
    return pl.pallas_call(_ssm_disc_kernel, out_shape=[shp1, shp1, shpb, shpb], name="ssm_discretize")(
        col(a_re), col(a_im), col(ldt), flat(b_re), flat(b_im))


SSM_CBLK = LANES // SSM_GROUP
SSM_NCB = SSM_WIDTH // LANES
SSM_TILES = SSM_COLS // LANES
SSM_CB_COLS = SSM_CBLK * SSM_STATE


def _ssm_kernel(u_ref, h0r_ref, h0i_ref, ar_ref, ai_ref, bre_ref, bim_ref, cre_ref, cim_ref, d_ref,
                y_ref, hr_ref, hi_ref, bu_ref, hs_ref, st_ref, *, tc, n_pass):
    ci = pl.program_id(1)
    nb = u_ref.shape[0]
    rows = nb * tc

    @pl.when(ci == 0)
    def _():
        for j in range(SSM_TILES):
            st_ref[j] = h0r_ref[:, j * LANES:(j + 1) * LANES]
            st_ref[SSM_TILES + j] = h0i_ref[:, j * LANES:(j + 1) * LANES]

    u = jnp.swapaxes(u_ref[...], 0, 1).reshape(rows, SSM_WIDTH)
    ub = u.astype(MXU_DTYPE)
    per_cb = SSM_CB_COLS // LANES
    for cb in range(SSM_NCB):
        ucb = ub[:, cb * LANES:(cb + 1) * LANES]
        re = jnp.dot(ucb, bre_ref[cb], preferred_element_type=F32)
        im = jnp.dot(ucb, bim_ref[cb], preferred_element_type=F32)
        for jj in range(per_cb):
            bu_ref[cb * per_cb + jj] = re[:, jj * LANES:(jj + 1) * LANES]
            bu_ref[SSM_TILES + cb * per_cb + jj] = im[:, jj * LANES:(jj + 1) * LANES]

    per_pass = SSM_TILES // n_pass
    for ps in range(n_pass):
        tiles = range(ps * per_pass, (ps + 1) * per_pass)
        ar = [ar_ref[:, j * LANES:(j + 1) * LANES] for j in tiles]
        ai = [ai_ref[:, j * LANES:(j + 1) * LANES] for j in tiles]

        def step(t, carry, tiles=tiles, ar=ar, ai=ai):
            out = []
            rows_t = pl.ds(pl.multiple_of(t * nb, nb), nb)
            for n, j in enumerate(tiles):
                hr, hi = carry[2 * n], carry[2 * n + 1]
                nr = ar[n] * hr - ai[n] * hi + bu_ref[j, rows_t, :]
                ni = ar[n] * hi + ai[n] * hr + bu_ref[SSM_TILES + j, rows_t, :]
                hs_ref[j, rows_t, :] = nr
                hs_ref[SSM_TILES + j, rows_t, :] = ni
                out += [nr, ni]
            return tuple(out)

        init = []
        for j in tiles:
            init += [st_ref[j], st_ref[SSM_TILES + j]]
        fin = lax.fori_loop(0, tc, step, tuple(init), unroll=2)
        for n, j in enumerate(tiles):
            st_ref[j] = fin[2 * n]
            st_ref[SSM_TILES + j] = fin[2 * n + 1]

    for cb in range(SSM_NCB):
        tiles = range(cb * per_cb, (cb + 1) * per_cb)
        h_re = jnp.concatenate([hs_ref[j] for j in tiles], axis=1).astype(MXU_DTYPE)
        h_im = jnp.concatenate([hs_ref[SSM_TILES + j] for j in tiles], axis=1).astype(MXU_DTYPE)
        acc = (d_ref[:, cb * LANES:(cb + 1) * LANES] * u[:, cb * LANES:(cb + 1) * LANES]
               + jnp.dot(h_re, cre_ref[cb], preferred_element_type=F32)
               + jnp.dot(h_im, cim_ref[cb], preferred_element_type=F32))
        y_ref[:, :, cb * LANES:(cb + 1) * LANES] = jnp.swapaxes(acc.reshape(tc, nb, LANES), 0, 1)

    @pl.when(ci == pl.num_programs(1) - 1)
    def _():
        for j in range(SSM_TILES):
            hr_ref[:, j * LANES:(j + 1) * LANES] = st_ref[j]
            hi_ref[:, j * LANES:(j + 1) * LANES] = st_ref[SSM_TILES + j]


def _ssm_block_weights(bbar_re, bbar_im, c_re, c_im):
    eye = jnp.eye(SSM_CBLK, dtype=F32)

    def b_blk(b):
        bt = b.reshape(SSM_NCB, SSM_CBLK, SSM_STATE, SSM_GROUP).transpose(0, 1, 3, 2)
        return (bt[:, :, :, None, :] * eye[None, :, None, :, None]).reshape(SSM_NCB, LANES, SSM_CB_COLS).astype(MXU_DTYPE)

    def c_blk(c):
        ct = c.reshape(SSM_NCB, SSM_CBLK, SSM_GROUP, SSM_STATE).transpose(0, 1, 3, 2)
        return (ct[:, :, :, None, :] * eye[None, :, None, :, None]).reshape(SSM_NCB, SSM_CB_COLS, LANES).astype(MXU_DTYPE)

    return b_blk(bbar_re), b_blk(bbar_im), c_blk(c_re), c_blk(-c_im)


def _ssm_scan(u, h0_re, h0_im, abar_re, abar_im, blocks, d_skip, tc, n_pass):
    nb_tot, t_tot, _ = u.shape
    nb = SUBLANES
    assert nb_tot % nb == 0 and t_tot % tc == 0 and tc % SUBLANES == 0
    bre, bim, cre, cim = blocks
    rows = nb * tc
    cst = lambda shape: pl.BlockSpec(shape, lambda g, c: (0,) * len(shape))
    st_spec = pl.BlockSpec((nb, SSM_COLS), lambda g, c: (g, 0))
    u_spec = pl.BlockSpec((nb, tc, SSM_WIDTH), lambda g, c: (g, c, 0))
    return pl.pallas_call(
        functools.partial(_ssm_kernel, tc=tc, n_pass=n_pass),
        grid=(nb_tot // nb, t_tot // tc),
        in_specs=[u_spec, st_spec, st_spec, cst((1, SSM_COLS)), cst((1, SSM_COLS)),
                  cst(bre.shape), cst(bim.shape), cst(cre.shape), cst(cim.shape), cst((1, SSM_WIDTH))],
        out_specs=[u_spec, st_spec, st_spec],
        out_shape=[jax.ShapeDtypeStruct(u.shape, F32),
                   jax.ShapeDtypeStruct((nb_tot, SSM_COLS), F32),
                   jax.ShapeDtypeStruct((nb_tot, SSM_COLS), F32)],
        scratch_shapes=[pltpu.VMEM((2 * SSM_TILES, rows, LANES), F32),
                        pltpu.VMEM((2 * SSM_TILES, rows, LANES), F32),
                        pltpu.VMEM((2 * SSM_TILES, nb, LANES), F32)],
        compiler_params=_cparams(("arbitrary", "arbitrary")),
        name="ssm_scan",
    )(u, h0_re, h0_im, abar_re, abar_im, bre, bim, cre, cim, d_skip)


def _rms(x, g):
    return x * lax.rsqrt(jnp.mean(x * x, axis=-1, keepdims=True) + EPS) * g


def _mxu(a, w_ref):
    return jnp.dot(a.astype(MXU_DTYPE), w_ref[...], preferred_element_type=F32)


def _tail_kernel(x_ref, attn_ref, ypre_ref, p_ref, gmix_ref, gffn_ref, gple_ref, gfin_ref,
                 wga_ref, wgb_ref, wa_ref, wglu_ref, bglu_ref, wb_ref, wout_ref,
                 wfg_ref, wfu_ref, wfd_ref, wpg_ref, wpp_ref, o_ref, *, ff_chunks, final_norm):
    x = x_ref[...]
    h = _rms(x, gmix_ref[...]).astype(MXU_DTYPE)
    branch_a = _mxu(attn_ref[...], wa_ref)
    y = jax.nn.gelu(ypre_ref[...])
    y = y * jax.nn.sigmoid(_mxu(y, wglu_ref) + bglu_ref[...])
    branch_b = _mxu(y, wb_ref)
    merged = (jax.nn.sigmoid(jnp.dot(h, wga_ref[...], preferred_element_type=F32)) * branch_a
              + jax.nn.sigmoid(jnp.dot(h, wgb_ref[...], preferred_element_type=F32)) * branch_b)
    x = x + _mxu(merged, wout_ref)
    h = _rms(x, gffn_ref[...]).astype(MXU_DTYPE)
    d_ff = wfg_ref.shape[1]
    fc = d_ff // ff_chunks
    ffn = jnp.zeros_like(x)
    for c in range(ff_chunks):
        gate = jnp.dot(h, wfg_ref[:, c * fc:(c + 1) * fc], preferred_element_type=F32)
        up = jnp.dot(h, wfu_ref[:, c * fc:(c + 1) * fc], preferred_element_type=F32)
        act = (jax.nn.silu(gate) * up).astype(MXU_DTYPE)
        ffn = ffn + jnp.dot(act, wfd_ref[c * fc:(c + 1) * fc, :], preferred_element_type=F32)
    x = x + ffn
    h = _rms(x, gple_ref[...])
    x = x + jax.nn.sigmoid(_mxu(h, wpg_ref)) * _mxu(p_ref[...], wpp_ref)
    o_ref[...] = _rms(x, gfin_ref[...]) if final_norm else x


def _tail(x, attn, ypre, p, gains, weights, tm, final_norm, ff_chunks=2):
    rows, d = x.shape
    assert rows % tm == 0
    row = lambda a: pl.BlockSpec((tm, a.shape[1]), lambda i: (i, 0))
    cst = lambda a: pl.BlockSpec(a.shape, lambda i: (0, 0), pipeline_mode=pl.Buffered(1))
    return pl.pallas_call(
        functools.partial(_tail_kernel, ff_chunks=ff_chunks, final_norm=final_norm),
        grid=(rows // tm,),
        in_specs=[row(x), row(attn), row(ypre), row(p)] + [cst(g) for g in gains] + [cst(w) for w in weights],
        out_specs=row(x),
        out_shape=jax.ShapeDtypeStruct(x.shape, F32),
        compiler_params=_cparams(("arbitrary",)),
        name="merge_ffn_ple",
    )(x, attn, ypre, p, *gains, *weights)


SCORE_CHUNK = 1024
KV_RING = 32
PAGE_UNROLL = 8


def _sample_attn_kernel(pt_ref, q_ref, qi_ref, kn_ref, vn_ref, kiwi_ref, kidx_hbm, ck_hbm, cv_hbm, o_ref,
                        kib_ref, kv_ref, sc_ref, sel_ref, lg_ref, acc_ref, tie_ref, kisem, kvsem,
                        *, group, t_new, n_pages, n_sel, n_steps):
    step = pl.program_id(0)
    past = n_pages * PAGE_SIZE
    width = past + LANES
    rows = group * t_new
    hq = N_HEADS * t_new
    chunks_per_seq = 2 * n_pages
    total_chunks = n_steps * group * chunks_per_seq
    lookahead = KV_RING - 1

    def ki_copy(s, g, page):
        src = kidx_hbm.at[pt_ref[(s * group + g) * n_pages + page]]
        dst = kib_ref.at[s % 2, g, :, pl.ds(pl.multiple_of(page * PAGE_SIZE, PAGE_SIZE), PAGE_SIZE)]
        return pltpu.make_async_copy(src, dst, kisem.at[s % 2])

    def for_ki_pages(s, fn):
        for g in range(group):
            def body(page, c, g=g):
                fn(ki_copy(s, g, page))
                return c
            lax.fori_loop(0, n_pages, body, 0)

    def kv_copy(gc, src_hbm):
        seq = gc // chunks_per_seq
        page = gc % n_pages
        return pltpu.make_async_copy(src_hbm.at[pt_ref[seq * n_pages + page]], kv_ref.at[gc % KV_RING],
                                     kvsem.at[gc % KV_RING])

    def issue_kv(gc):
        is_k = (gc % chunks_per_seq) < n_pages

        @pl.when((gc < total_chunks) & is_k)
        def _():
            kv_copy(gc, ck_hbm).start()

        @pl.when((gc < total_chunks) & jnp.logical_not(is_k))
        def _():
            kv_copy(gc, cv_hbm).start()

    @pl.when(step == 0)
    def _():
        for_ki_pages(step, lambda cp: cp.start())
        for c in range(lookahead):
            issue_kv(jnp.int32(c))

    for_ki_pages(step, lambda cp: cp.wait())

    @pl.when(step + 1 < n_steps)
    def _():
        for_ki_pages(step + 1, lambda cp: cp.start())

    lane1 = lax.broadcasted_iota(I32, (t_new, LANES), 1)
    row1 = lax.broadcasted_iota(I32, (t_new, LANES), 0)
    new_visible = lane1 <= row1

    def pad_rows(a):
        return jnp.concatenate([a, jnp.zeros((LANES - t_new, a.shape[1]), a.dtype)], axis=0)

    qi_hq, w_col = [], []
    for g in range(group):
        qi_g = qi_ref[g * t_new:(g + 1) * t_new, :].astype(MXU_DTYPE)
        qi_hq.append(jnp.concatenate([qi_g[:, h * IDX_DIM:(h + 1) * IDX_DIM] for h in range(IDX_HEADS)], axis=0))
        w_col.append([kiwi_ref[g * t_new:(g + 1) * t_new, IDX_DIM + h:IDX_DIM + h + 1] for h in range(IDX_HEADS)])

    def head_sum(s, w):
        acc = jnp.zeros((t_new, s.shape[1]), F32)
        for h in range(IDX_HEADS):
            acc = acc + jnp.maximum(s[h * t_new:(h + 1) * t_new, :], 0.0) * w[h]
        return acc

    def score_chunk(c, carry):
        off = pl.multiple_of(c * SCORE_CHUNK, SCORE_CHUNK)
        for g in range(group):
            kc = kib_ref[step % 2, g, :, pl.ds(off, SCORE_CHUNK)].astype(MXU_DTYPE)
            s = jnp.dot(qi_hq[g], kc, preferred_element_type=F32)
            sc_ref[g * t_new:(g + 1) * t_new, pl.ds(off, SCORE_CHUNK)] = head_sum(s, w_col[g])
        return carry

    lax.fori_loop(0, past // SCORE_CHUNK, score_chunk, 0)
    for g in range(group):
        ki_new = pad_rows(kiwi_ref[g * t_new:(g + 1) * t_new, :IDX_DIM]).astype(MXU_DTYPE)
        s_new = jnp.where(new_visible, head_sum(_nt_dot(qi_hq[g], ki_new), w_col[g]), NEG)
        sc_ref[g * t_new:(g + 1) * t_new, past:width] = jnp.where(lane1 < t_new, s_new, -jnp.inf)

    n_tiles = width // LANES
    n_acc = 4

    def count(pred):
        accs = [jnp.zeros((rows, LANES), I32)] * n_acc
        for j in range(n_tiles):
            accs[j % n_acc] = accs[j % n_acc] + pred(sc_ref[:, j * LANES:(j + 1) * LANES], j * LANES).astype(I32)
        return jnp.sum((accs[0] + accs[1]) + (accs[2] + accs[3]), axis=1, keepdims=True)

    thr = _kth_largest(lambda cand, key: count(lambda kk, off: kk >= cand), n_sel, (rows, 1))
    cnt_ge = count(lambda kk, off: kk >= thr)
    need = n_sel - count(lambda kk, off: kk > thr)

    idx_bits = (width - 1).bit_length()
    lane_r = lax.broadcasted_iota(I32, (rows, LANES), 1)
    tie_ref[...] = jnp.full((rows, 1), (1 << idx_bits) - 1, I32)

    @pl.when(jnp.max(cnt_ge) > n_sel)
    def _():
        def idx_step(i, x):
            cand = x | jnp.left_shift(jnp.int32(1), idx_bits - 1 - i)
            c = count(lambda kk, off: (kk == thr) & ((lane_r + off) < cand))
            return jnp.where(c < need, cand, x)
        tie_ref[...] = lax.fori_loop(0, idx_bits, idx_step, jnp.zeros((rows, 1), I32))

    tie_x = tie_ref[...]
    new_vis_rows = jnp.concatenate([new_visible] * group, axis=0)
    for j in range(n_tiles):
        kk = sc_ref[:, j * LANES:(j + 1) * LANES]
        taken = jnp.where(kk > thr, 1, jnp.where(kk == thr, jnp.where((lane_r + j * LANES) <= tie_x, 1, 0), 0))
        if j == n_tiles - 1:
            taken = jnp.where(new_vis_rows, taken, 0)
        sel_ref[:, j * LANES:(j + 1) * LANES] = taken.astype(F32)

    head_of_row = lax.broadcasted_iota(I32, (hq, ATTN_WIDTH), 0) // t_new
    head_of_lane = lax.broadcasted_iota(I32, (hq, ATTN_WIDTH), 1) // HEAD_DIM
    for g in range(group):
        r0 = g * t_new
        chunk0 = (step * group + g) * chunks_per_seq
        q_g = q_ref[r0:r0 + t_new, :]
        q_bd = jnp.where(head_of_row == head_of_lane, jnp.concatenate([q_g] * N_HEADS, axis=0), 0.0).astype(MXU_DTYPE)

        def sel_hq(lo, r0=r0):
            s = sel_ref[r0:r0 + t_new, pl.ds(lo, LANES)]
            return jnp.concatenate([s] * N_HEADS, axis=0) > 0.0

        def k_pages(it, m, chunk0=chunk0, q_bd=q_bd, sel_hq=sel_hq):
            gc0 = chunk0 + it * PAGE_UNROLL
            for un in range(PAGE_UNROLL):
                kv_copy(gc0 + un, ck_hbm).wait()
            for un in range(PAGE_UNROLL):
                kp = kv_ref[(gc0 + un) % KV_RING].astype(MXU_DTYPE)
                off = pl.multiple_of((it * PAGE_UNROLL + un) * PAGE_SIZE, PAGE_SIZE)
                lg = jnp.where(sel_hq(off), jnp.dot(q_bd, kp, preferred_element_type=F32), NEG)
                lg_ref[:, pl.ds(off, LANES)] = lg
                m = jnp.maximum(m, lg)
            for un in range(PAGE_UNROLL):
                issue_kv(gc0 + un + lookahead)
            return m

        m = lax.fori_loop(0, n_pages // PAGE_UNROLL, k_pages, jnp.full((hq, LANES), -jnp.inf, F32))
        k_new = pad_rows(kn_ref[r0:r0 + t_new, :]).astype(MXU_DTYPE)
        lg_new = jnp.where(sel_hq(past), _nt_dot(q_bd, k_new), NEG)
        lg_ref[:, past:width] = lg_new
        m = jnp.max(jnp.maximum(m, lg_new), axis=1, keepdims=True)

        acc_ref[...] = jnp.zeros_like(acc_ref)

        def v_pages(it, den, chunk0=chunk0, m=m):
            gc0 = chunk0 + n_pages + it * PAGE_UNROLL
            for un in range(PAGE_UNROLL):
                kv_copy(gc0 + un, ck_hbm).wait()
            acc = acc_ref[...]
            for un in range(PAGE_UNROLL):
                vp = kv_ref[(gc0 + un) % KV_RING].astype(MXU_DTYPE)
                off = pl.multiple_of((it * PAGE_UNROLL + un) * PAGE_SIZE, PAGE_SIZE)
                e = jnp.exp(lg_ref[:, pl.ds(off, LANES)] - m)
                acc = acc + _nt_dot(e.astype(MXU_DTYPE), vp)
                den = den + e
            acc_ref[...] = acc
            for un in range(PAGE_UNROLL):
                issue_kv(gc0 + un + lookahead)
            return den

        den = lax.fori_loop(0, n_pages // PAGE_UNROLL, v_pages, jnp.zeros((hq, LANES), F32))
        e_new = jnp.exp(lg_ref[:, past:width] - m)
        v_new = pad_rows(vn_ref[r0:r0 + t_new, :]).astype(MXU_DTYPE)
        acc = acc_ref[...] + jnp.dot(e_new.astype(MXU_DTYPE), v_new, preferred_element_type=F32)
        o = acc / jnp.sum(den + e_new, axis=1, keepdims=True)
        o = jnp.where(head_of_row == head_of_lane, o, 0.0)
        out = o[0:t_new, :]
        for h in range(1, N_HEADS):
            out = out + o[h * t_new:(h + 1) * t_new, :]
        o_ref[r0:r0 + t_new, :] = out


def _sample_attention(q, qi, k_new, v_new, kiwi, kidx_pages, k_pages, v_pages, page_table, t_new, group):
    rows_tot = q.shape[0]
    n_seq, n_pages = page_table.shape
    assert n_seq % group == 0 and (n_pages * PAGE_SIZE) % SCORE_CHUNK == 0 and t_new <= SUBLANES
    assert n_pages % PAGE_UNROLL == 0
    n_steps = n_seq // group
    rows = group * t_new
    past = n_pages * PAGE_SIZE
    width = past + LANES
    n_sel = min(TOPK_MAX, (past + t_new) // 4)
    row = lambda w: pl.BlockSpec((rows, w), lambda i, pt: (i, 0))
    hbm = pl.BlockSpec(memory_space=pl.ANY)
    grid_spec = pltpu.PrefetchScalarGridSpec(
        num_scalar_prefetch=1,
        grid=(n_steps,),
        in_specs=[row(ATTN_WIDTH), row(ATTN_WIDTH), row(ATTN_WIDTH), row(ATTN_WIDTH), row(LANES), hbm, hbm, hbm],
        out_specs=row(ATTN_WIDTH),
        scratch_shapes=[pltpu.VMEM((2, group, IDX_DIM, past), F32),
                        pltpu.VMEM((KV_RING, ATTN_WIDTH, PAGE_SIZE), F32),
                        pltpu.VMEM((rows, width), F32),
                        pltpu.VMEM((rows, width), F32),
                        pltpu.VMEM((N_HEADS * t_new, width), F32),
                        pltpu.VMEM((N_HEADS * t_new, ATTN_WIDTH), F32),
                        pltpu.VMEM((rows, 1), I32),
                        pltpu.SemaphoreType.DMA((2,)),
                        pltpu.SemaphoreType.DMA((KV_RING,))])
    return pl.pallas_call(
        functools.partial(_sample_attn_kernel, group=group, t_new=t_new, n_pages=n_pages, n_sel=n_sel,
                          n_steps=n_steps),
        grid_spec=grid_spec,
        out_shape=jax.ShapeDtypeStruct((rows_tot, ATTN_WIDTH), F32),
        compiler_params=_cparams(("arbitrary",)),
        name="sample_attention",
    )(page_table.reshape(-1), q, qi, k_new, v_new, kiwi, kidx_pages, k_pages, v_pages)


IN_SIZES = (ATTN_WIDTH, ATTN_WIDTH, ATTN_WIDTH, IDX_HEADS * IDX_DIM, IDX_DIM, IDX_HEADS, SSM_WIDTH)
PROMPT_ROW_TILE = 512
SAMPLE_GROUP = 4
SSM_TIME_CHUNK = 128
SSM_PASSES = 2


def _split_w_in(w):
    d = w.shape[0]
    offs = np.cumsum((0,) + IN_SIZES)
    o_ki, o_u, o_ga = int(offs[4]), int(offs[6]), int(offs[7])
    kiwi = jnp.pad(w[:, o_ki:o_u], ((0, 0), (0, LANES - (o_u - o_ki))))
    w_main = jnp.concatenate([w[:, :o_ki], kiwi, w[:, o_u:o_ga]], axis=1).astype(MXU_DTYPE)
    return w_main, w[:, o_ga:o_ga + d].astype(MXU_DTYPE), w[:, o_ga + d:o_ga + 2 * d].astype(MXU_DTYPE)


def kernel(x_prompt, x_sample, p_prompt, p_sample, cache_k, cache_v, cache_kidx, state_ssm_re, state_ssm_im, page_table, norm_mix, w_in, w_branch_a, w_branch_b, w_out, ssm_a_re, ssm_a_im, ssm_log_dt, ssm_b_re, ssm_b_im, ssm_c_re, ssm_c_im, ssm_d, w_glu, b_glu, norm_ffn, w_ff_gate, w_ff_up, w_ff_down, norm_ple, w_ple_gate, w_ple_proj, norm_final):
    b, s, d = x_prompt.shape
    db, t, _ = x_sample.shape
    depth = w_in.shape[0]
    n_pages = page_table.shape[1]
    past = n_pages * PAGE_SIZE
    n_phys = cache_k.shape[1]
    rows_p, rows_s = b * s, db * t
    tm_p = min(PROMPT_ROW_TILE, s)
    tm_s = min(PROMPT_ROW_TILE, rows_s)
    tables_p = _rope_tables(s, 0, s)
    tables_s = _rope_tables(tm_s, past, t)
    bf = lambda a: a.astype(MXU_DTYPE)
    row = lambda a: a.reshape(1, -1)

    xp = x_prompt.reshape(rows_p, d)
    xs = x_sample.reshape(rows_s, d)
    st_p, st_s = [], []
    for i in range(depth):
        w_main, w_ga, w_gb = _split_w_in(w_in[i])
        abr, abi, bbr, bbi = _ssm_discretize(ssm_a_re[i], ssm_a_im[i], ssm_log_dt[i], ssm_b_re[i], ssm_b_im[i])
        blocks = _ssm_block_weights(bbr, bbi, ssm_c_re[i], ssm_c_im[i])
        gains = [row(norm_mix[i]), row(norm_ffn[i]), row(norm_ple[i]), row(norm_final)]
        weights = [w_ga, w_gb, bf(w_branch_a[i]), bf(w_glu[i]), row(b_glu[i]), bf(w_branch_b[i]), bf(w_out[i]),
                   bf(w_ff_gate[i]), bf(w_ff_up[i]), bf(w_ff_down[i]), bf(w_ple_gate[i]), bf(w_ple_proj[i])]
        final = i == depth - 1

        (q, kbf, kt, _, vt, vtbf, qi, _, kiwit, kidup, u) = _inproj(
            xp.reshape(b, s, d), gains[0], w_main, tables_p, tm_p)
        attn = _prompt_attention(q, qi, kiwit, kbf, kidup, vtbf, b, s)
        zeros = jnp.zeros((b, SSM_COLS), F32)
        ypre, hr, hi = _ssm_scan(u.reshape(b, s, SSM_WIDTH), zeros, zeros, row(abr), row(abi), blocks,
                                 row(ssm_d[i]), min(SSM_TIME_CHUNK, s), SSM_PASSES)
        xp = _tail(xp, attn, ypre.reshape(rows_p, SSM_WIDTH), p_prompt[i].reshape(rows_p, -1), gains, weights,
                   tm_p, final)
        heads = lambda a: a.reshape(b, N_HEADS, HEAD_DIM, s).transpose(0, 3, 1, 2)
        st_p.append((heads(kt), heads(vt), kiwit[:, :IDX_DIM, :].transpose(0, 2, 1),
                     hr.reshape(b, SSM_GROUPS, SSM_STATE), hi.reshape(b, SSM_GROUPS, SSM_STATE)))

        (q, kbf, kt, vbf, vt, _, qi, kiwi, _, _, u) = _inproj(
            xs.reshape(1, rows_s, d), gains[0], w_main, tables_s, tm_s)
        f32 = lambda a: a.astype(F32)
        attn = _sample_attention(
            f32(q), f32(qi), f32(kbf), f32(vbf), kiwi,
            cache_kidx[i].transpose(0, 2, 1),
            cache_k[i].transpose(0, 2, 3, 1).reshape(n_phys, ATTN_WIDTH, PAGE_SIZE),
            cache_v[i].transpose(0, 2, 3, 1).reshape(n_phys, ATTN_WIDTH, PAGE_SIZE),
            page_table, t, SAMPLE_GROUP)
        ypre, hr, hi = _ssm_scan(u.reshape(db, t, SSM_WIDTH), state_ssm_re[i].reshape(db, SSM_COLS),
                                 state_ssm_im[i].reshape(db, SSM_COLS), row(abr), row(abi), blocks,
                                 row(ssm_d[i]), t, SSM_PASSES)
        xs = _tail(xs, attn, ypre.reshape(rows_s, SSM_WIDTH), p_sample[i].reshape(rows_s, -1), gains, weights,
                   tm_s, final)
        heads = lambda a: a.reshape(N_HEADS, HEAD_DIM, db, t).transpose(2, 3, 0, 1)
        st_s.append((heads(kt), heads(vt), kiwi[:, :IDX_DIM].reshape(db, t, IDX_DIM),
                     hr.reshape(db, SSM_GROUPS, SSM_STATE), hi.reshape(db, SSM_GROUPS, SSM_STATE)))

    stack = lambda sts, j: jnp.stack([st[j] for st in sts])
    return (xp.reshape(b, s, d), xs.reshape(db, t, d),
            *[stack(st_p, j) for j in range(5)], *[stack(st_s, j) for j in range(5)])
```

```python
import functools
import math

import numpy as np
import jax
import jax.numpy as jnp
from jax import lax
from jax.experimental import pallas as pl
from jax.experimental.pallas import tpu as pltpu

N_HEADS = 8
HEAD_DIM = 64
ATTN_WIDTH = N_HEADS * HEAD_DIM
IDX_HEADS = 8
IDX_DIM = 64
TOPK_MAX = 256
ROPE_THETA = 500000.0
ROPE_ROT = HEAD_DIM // 4
ROPE_HALF = ROPE_ROT // 2
SSM_GROUP = 16
SSM_GROUPS = 32
SSM_WIDTH = SSM_GROUP * SSM_GROUPS
SSM_STATE = 64
SSM_COLS = SSM_GROUPS * SSM_STATE
PAGE_SIZE = 128
Q_BLOCK = 128
EPS = 1e-6
NEG = -1e30

LANES = 128
SUBLANES = 8
VMEM_LIMIT = 56 * 1024 * 1024

MXU_DTYPE = jnp.bfloat16
F32 = jnp.float32
I32 = jnp.int32
INT_MIN = -(2 ** 31)


def _key_to_f32(key):
    return pltpu.bitcast(key ^ ((key >> 31) & 0x7FFFFFFF), F32)


def _kth_largest(count_ge, n_sel, shape):
    zero = jnp.zeros(shape, I32)
    base = jnp.where(count_ge(_key_to_f32(zero), zero) >= n_sel, 0, INT_MIN).astype(I32)

    def bit_step(i, base):
        cand = base | jnp.left_shift(jnp.int32(1), 30 - i)
        return jnp.where(count_ge(_key_to_f32(cand), cand) >= n_sel, cand, base)

    return _key_to_f32(lax.fori_loop(0, 31, bit_step, base))


def _cparams(sem):
    return pltpu.CompilerParams(dimension_semantics=sem, vmem_limit_bytes=VMEM_LIMIT)


def _nt_dot(a, b):
    return lax.dot_general(a, b, (((1,), (1,)), ((), ())), preferred_element_type=F32)


def _rope_table_kernel(invf_ref, c_ref, s1_ref, s2_ref, *, pos0, period, rows):
    i = pl.program_id(0)
    r = lax.broadcasted_iota(I32, (rows, LANES), 0) + i * rows
    pos = (pos0 + r % period).astype(F32)
    ang = pos * invf_ref[...]
    lane = lax.broadcasted_iota(I32, (rows, LANES), 1) % HEAD_DIM
    c = jnp.cos(ang)
    s = jnp.sin(ang)
    c_ref[...] = jnp.where(lane < ROPE_ROT, c, 1.0)
    s1_ref[...] = jnp.where(lane < ROPE_HALF, -s, 0.0)
    s2_ref[...] = jnp.where((lane >= ROPE_HALF) & (lane < ROPE_ROT), s, 0.0)


def _rope_tables(n_rows, pos0, period):
    inv_freq = ROPE_THETA ** (-jnp.arange(0, ROPE_ROT, 2, dtype=F32) / ROPE_ROT)
    lane = np.arange(LANES) % HEAD_DIM
    invf = jnp.where(lane < ROPE_ROT, inv_freq[lane % ROPE_HALF], 0.0).astype(F32)[None, :]
    rows = min(n_rows, 512)
    assert n_rows % rows == 0
    spec = pl.BlockSpec((rows, LANES), lambda i: (i, 0))
    return pl.pallas_call(
        functools.partial(_rope_table_kernel, pos0=pos0, period=period, rows=rows),
        grid=(n_rows // rows,),
        in_specs=[pl.BlockSpec((1, LANES), lambda i: (0, 0))],
        out_specs=[spec, spec, spec],
        out_shape=[jax.ShapeDtypeStruct((n_rows, LANES), F32)] * 3,
        compiler_params=_cparams(("arbitrary",)),
        name="rope_tables",
    )(invf)


W_Q, W_K, W_V, W_QI, W_KIWI, W_U = 0, 512, 1024, 1536, 2048, 2176
W_MAIN = 2688


def _inproj_kernel(x_ref, g_ref, w_ref, c_ref, s1_ref, s2_ref,
                   q_ref, kbf_ref, kt_ref, vbf_ref, vt_ref, vtbf_ref, qi_ref,
                   kiwi_ref, kiwit_ref, kidup_ref, u_ref):
    x = x_ref[...]
    ms = jnp.mean(x * x, axis=-1, keepdims=True)
    h = (x * lax.rsqrt(ms + EPS) * g_ref[...]).astype(MXU_DTYPE)
    cos_t, sin_lo, sin_hi = c_ref[...], s1_ref[...], s2_ref[...]

    def proj(lo, hi):
        return jnp.dot(h, w_ref[:, lo:hi], preferred_element_type=F32)

    def rope_tile(t):
        return t * cos_t + pltpu.roll(t, LANES - ROPE_HALF, 1) * sin_lo + pltpu.roll(t, ROPE_HALF, 1) * sin_hi

    def rope(z):
        return jnp.concatenate([rope_tile(z[:, j * LANES:(j + 1) * LANES]) for j in range(z.shape[1] // LANES)], axis=1)

    q_ref[...] = (rope(proj(W_Q, W_K)) * (HEAD_DIM ** -0.5)).astype(q_ref.dtype)
    k = rope(proj(W_K, W_V))
    kbf_ref[...] = k.astype(kbf_ref.dtype)
    kt_ref[...] = k.T
    v = proj(W_V, W_QI)
    vbf_ref[...] = v.astype(vbf_ref.dtype)
    vt = v.T
    vt_ref[...] = vt
    vtbf_ref[...] = vt.astype(vtbf_ref.dtype)
    qi_ref[...] = (rope(proj(W_QI, W_KIWI)) * (IDX_DIM ** -0.5)).astype(qi_ref.dtype)
    z = proj(W_KIWI, W_U)
    lane = lax.broadcasted_iota(I32, z.shape, 1)
    kiwi = jnp.where(lane < IDX_DIM, rope_tile(z), z * (IDX_HEADS ** -0.5))
    kiwi_ref[...] = kiwi
    kiwit_ref[...] = kiwi.T
    kidup_ref[...] = jnp.where(lane < IDX_DIM, kiwi, pltpu.roll(kiwi, IDX_DIM, 1)).astype(kidup_ref.dtype)
    u_ref[...] = proj(W_U, W_MAIN)


def _inproj(x, gain, w_main, tables, tm):
    bk, sk, d = x.shape
    nt = sk // tm
    assert sk % tm == 0
    cos_t, sin_lo, sin_hi = tables
    n_tab = cos_t.shape[0] // tm
    rows = bk * sk
    xf = x.reshape(rows, d)
    row = lambda w: pl.BlockSpec((tm, w), lambda i: (i, 0))
    tr = lambda w: pl.BlockSpec((None, w, tm), lambda i: (i // nt, 0, i % nt))
    tab = pl.BlockSpec((tm, LANES), lambda i: (i % n_tab, 0))
    outs = [
        (row(ATTN_WIDTH), (rows, ATTN_WIDTH), MXU_DTYPE),
        (row(ATTN_WIDTH), (rows, ATTN_WIDTH), MXU_DTYPE),
        (tr(ATTN_WIDTH), (bk, ATTN_WIDTH, sk), F32),
        (row(ATTN_WIDTH), (rows, ATTN_WIDTH), MXU_DTYPE),
        (tr(ATTN_WIDTH), (bk, ATTN_WIDTH, sk), F32),
        (tr(ATTN_WIDTH), (bk, ATTN_WIDTH, sk), MXU_DTYPE),
        (row(ATTN_WIDTH), (rows, ATTN_WIDTH), MXU_DTYPE),
        (row(LANES), (rows, LANES), F32),
        (tr(LANES), (bk, LANES, sk), F32),
        (row(LANES), (rows, LANES), MXU_DTYPE),
        (row(SSM_WIDTH), (rows, SSM_WIDTH), F32),
    ]
    return pl.pallas_call(
        _inproj_kernel,
        grid=(rows // tm,),
        in_specs=[row(d),
                  pl.BlockSpec((1, d), lambda i: (0, 0)),
                  pl.BlockSpec((d, W_MAIN), lambda i: (0, 0)),
                  tab, tab, tab],
        out_specs=[o[0] for o in outs],
        out_shape=[jax.ShapeDtypeStruct(o[1], o[2]) for o in outs],
        compiler_params=_cparams(("arbitrary",)),
        name="inproj",
    )(xf, gain, w_main, cos_t, sin_lo, sin_hi)


KEY_CHUNK = 512


def _head_masked_pair(tile, lane_lo):
    zero = jnp.zeros_like(tile)
    return jnp.concatenate([jnp.where(lane_lo, tile, zero), jnp.where(lane_lo, zero, tile)], axis=0)


def _prompt_attn_kernel(q_ref, qi_ref, kw_ref, k_ref, kidup_ref, vt_ref, o_ref,
                        sc_ref, lg_ref, ot_ref, rhs_ref, wi_ref, tie_ref, *, seq, n_sel):
    qb = pl.program_id(1)
    lane_lo = lax.broadcasted_iota(I32, (Q_BLOCK, LANES), 1) < HEAD_DIM
    lane_q = lax.broadcasted_iota(I32, (1, Q_BLOCK), 1)
    row_iota = lax.broadcasted_iota(I32, (KEY_CHUNK, Q_BLOCK), 0)
    n_pairs = N_HEADS // 2
    groups = KEY_CHUNK // SUBLANES

    def n_chunks(blk):
        return ((blk + 1) * Q_BLOCK + KEY_CHUNK - 1) // KEY_CHUNK

    nch = n_chunks(qb)
    n_virtual = seq - nch * KEY_CHUNK
    qpos = qb * Q_BLOCK + lane_q

    def chunk_rows(c):
        return pl.ds(pl.multiple_of(c * KEY_CHUNK, KEY_CHUNK), KEY_CHUNK)

    def group_reduce(op, a):
        return op(a.reshape(groups, SUBLANES, a.shape[1]), axis=0)

    for p in range(n_pairs):
        rhs_ref[p] = _head_masked_pair(qi_ref[:, p * LANES:(p + 1) * LANES], lane_lo)
        rhs_ref[n_pairs + p] = _head_masked_pair(q_ref[:, p * LANES:(p + 1) * LANES], lane_lo)
    wi_ref[...] = kw_ref[IDX_DIM:IDX_DIM + IDX_HEADS, :]

    def produce_chunk(c, carry):
        rows = chunk_rows(c)
        kd = kidup_ref[rows, :]
        acc = jnp.zeros((KEY_CHUNK, Q_BLOCK), F32)
        for p in range(n_pairs):
            s = _nt_dot(kd, rhs_ref[p])
            acc = acc + jnp.maximum(s[:, :Q_BLOCK], 0.0) * wi_ref[2 * p:2 * p + 1, :]
            acc = acc + jnp.maximum(s[:, Q_BLOCK:], 0.0) * wi_ref[2 * p + 1:2 * p + 2, :]
        visible = (row_iota + c * KEY_CHUNK) <= qpos
        sc_ref[rows, :] = jnp.where(visible, acc, NEG)
        for p in range(n_pairs):
            lg = _nt_dot(k_ref[rows, p * LANES:(p + 1) * LANES], rhs_ref[n_pairs + p])
            lg_ref[p, rows, :] = lg.astype(lg_ref.dtype)
        return carry

    lax.fori_loop(0, nch, produce_chunk, 0)

    def count(pred):
        def body(c, acc):
            m = pred(sc_ref[chunk_rows(c), :], c * KEY_CHUNK).astype(I32)
            return acc + group_reduce(jnp.sum, m)
        acc = lax.fori_loop(0, nch, body, jnp.zeros((SUBLANES, Q_BLOCK), I32))
        return jnp.sum(acc, axis=0, keepdims=True)

    def count_ge(cand, key):
        return count(lambda sc, off: sc >= cand) + jnp.where(cand <= NEG, n_virtual, 0)

    thr = _kth_largest(count_ge, n_sel, (1, Q_BLOCK))
    cnt_ge = count_ge(thr, None)
    cnt_gt = count(lambda sc, off: sc > thr) + jnp.where(thr < NEG, n_virtual, 0)
    need = n_sel - cnt_gt

    idx_bits = max(1, (seq - 1).bit_length())
    tie_ref[...] = jnp.full((1, Q_BLOCK), (1 << idx_bits) - 1, I32)

    @pl.when(jnp.max(cnt_ge) > n_sel)
    def _():
        def idx_step(i, x):
            cand = x | jnp.left_shift(jnp.int32(1), idx_bits - 1 - i)
            c = count(lambda kk, off: (kk == thr) & ((row_iota + off) < cand))
            return jnp.where(c < need, cand, x)
        tie_ref[...] = lax.fori_loop(0, idx_bits, idx_step, jnp.zeros((1, Q_BLOCK), I32))

    tie_x = tie_ref[...]

    def mask_chunk(c, ms):
        rows = chunk_rows(c)
        sc = sc_ref[rows, :]
        idx = row_iota + c * KEY_CHUNK
        taken = jnp.where(sc > thr, 1, jnp.where(sc == thr, jnp.where(idx <= tie_x, 1, 0), 0))
        sel = jnp.where(idx <= qpos, taken, 0) > 0
        out = []
        for p in range(n_pairs):
            lo = jnp.where(sel, lg_ref[p, rows, :Q_BLOCK], NEG)
            hi = jnp.where(sel, lg_ref[p, rows, Q_BLOCK:], NEG)
            lg_ref[p, rows, :Q_BLOCK] = lo
            lg_ref[p, rows, Q_BLOCK:] = hi
            out.append(jnp.maximum(ms[2 * p], group_reduce(jnp.max, lo)))
            out.append(jnp.maximum(ms[2 * p + 1], group_reduce(jnp.max, hi)))
        return tuple(out)

    ms = lax.fori_loop(0, nch, mask_chunk, (jnp.full((SUBLANES, Q_BLOCK), -jnp.inf, F32),) * N_HEADS)
    ms = [jnp.max(m, axis=0, keepdims=True) for m in ms]

    ot_ref[...] = jnp.zeros_like(ot_ref)

    def pv_chunk(c, dens):
        rows = chunk_rows(c)
        out = []
        for h in range(N_HEADS):
            half = slice((h % 2) * Q_BLOCK, (h % 2 + 1) * Q_BLOCK)
            e = jnp.exp(lg_ref[h // 2, rows, half] - ms[h])
            out.append(dens[h] + group_reduce(jnp.sum, e))
            v = vt_ref[h * HEAD_DIM:(h + 1) * HEAD_DIM, rows]
            ot_ref[h * HEAD_DIM:(h + 1) * HEAD_DIM, :] += jnp.dot(v, e.astype(MXU_DTYPE), preferred_element_type=F32)
        return tuple(out)

    dens = lax.fori_loop(0, nch, pv_chunk, (jnp.zeros((SUBLANES, Q_BLOCK), F32),) * N_HEADS)
    for h in range(N_HEADS):
        den = jnp.sum(dens[h], axis=0, keepdims=True)
        ot_ref[h * HEAD_DIM:(h + 1) * HEAD_DIM, :] = ot_ref[h * HEAD_DIM:(h + 1) * HEAD_DIM, :] / den
    o_ref[...] = ot_ref[...].T.astype(o_ref.dtype)


def _prompt_attention(q, qi, kiwit, k, kidup, vt, batch, seq):
    nqb = seq // Q_BLOCK
    assert seq % KEY_CHUNK == 0
    n_sel = min(TOPK_MAX, seq // 4)
    qspec = pl.BlockSpec((Q_BLOCK, ATTN_WIDTH), lambda b, j: (b * nqb + j, 0))
    return pl.pallas_call(
        functools.partial(_prompt_attn_kernel, seq=seq, n_sel=n_sel),
        grid=(batch, nqb),
        in_specs=[qspec, qspec,
                  pl.BlockSpec((None, LANES, Q_BLOCK), lambda b, j: (b, 0, j)),
                  pl.BlockSpec((seq, ATTN_WIDTH), lambda b, j: (b, 0)),
                  pl.BlockSpec((seq, LANES), lambda b, j: (b, 0)),
                  pl.BlockSpec((None, ATTN_WIDTH, seq), lambda b, j: (b, 0, 0))],
        out_specs=qspec,
        out_shape=jax.ShapeDtypeStruct((batch * seq, ATTN_WIDTH), MXU_DTYPE),
        scratch_shapes=[pltpu.VMEM((seq, Q_BLOCK), F32),
                        pltpu.VMEM((N_HEADS // 2, seq, 2 * Q_BLOCK), F32),
                        pltpu.VMEM((ATTN_WIDTH, Q_BLOCK), F32),
                        pltpu.VMEM((N_HEADS, 2 * Q_BLOCK, LANES), MXU_DTYPE),
                        pltpu.VMEM((IDX_HEADS, Q_BLOCK), F32),
                        pltpu.VMEM((1, Q_BLOCK), I32)],
        compiler_params=_cparams(("arbitrary", "arbitrary")),
        name="prompt_attention",
    )(q, qi, kiwit, k, kidup, vt)


def _ssm_disc_kernel(are_ref, aim_ref, ldt_ref, bre_ref, bim_ref, abr_ref, abi_ref, bbr_ref, bbi_ref):
    are, aim = are_ref[...], aim_ref[...]
    dt = jnp.exp(ldt_ref[...])
    mag = jnp.exp(are * dt)
    abr = mag * jnp.cos(aim * dt)
    abi = mag * jnp.sin(aim * dt)
    abr_ref[...] = abr
    abi_ref[...] = abi
    nr, ni = abr - 1.0, abi
    den = are * are + aim * aim
    fr = (nr * are + ni * aim) / den
    fi = (ni * are - nr * aim) / den
    bre, bim = bre_ref[...], bim_ref[...]
    bbr_ref[...] = fr * bre - fi * bim
    bbi_ref[...] = fr * bim + fi * bre


def _ssm_discretize(a_re, a_im, log_dt, b_re, b_im):
    col = lambda a: a.reshape(SSM_COLS, 1)
    ldt = jnp.broadcast_to(log_dt[:, None], (SSM_GROUPS, SSM_STATE))
    flat = lambda b: b.reshape(SSM_COLS, SSM_GROUP)
    shp1 = jax.ShapeDtypeStruct((SSM_COLS, 1), F32)
    shpb = jax.ShapeDtypeStruct((SSM_COLS, SSM_GROUP), F32)
    return pl.pallas_call(_ssm_disc_kernel, out_shape=[shp1, shp1, shpb, shpb], name="ssm_discretize")(
        col(a_re), col(a_im), col(ldt), flat(b_re), flat(b_im))


SSM_CBLK = LANES // SSM_GROUP
SSM_NCB = SSM_WIDTH // LANES
SSM_TILES = SSM_COLS // LANES
SSM_CB_COLS = SSM_CBLK * SSM_STATE


def _ssm_kernel(u_ref, h0r_ref, h0i_ref, ar_ref, ai_ref, bre_ref, bim_ref, cre_ref, cim_ref, d_ref,
                y_ref, hr_ref, hi_ref, bu_ref, hs_ref, st_ref, *, tc, n_pass):
    ci = pl.program_id(1)
    nb = u_ref.shape[0]
    rows = nb * tc

    @pl.when(ci == 0)
    def _():
        for j in range(SSM_TILES):
            st_ref[j] = h0r_ref[:, j * LANES:(j + 1) * LANES]
            st_ref[SSM_TILES + j] = h0i_ref[:, j * LANES:(j + 1) * LANES]

    u = jnp.swapaxes(u_ref[...], 0, 1).reshape(rows, SSM_WIDTH)
    ub = u.astype(MXU_DTYPE)
    per_cb = SSM_CB_COLS // LANES
    for cb in range(SSM_NCB):
        ucb = ub[:, cb * LANES:(cb + 1) * LANES]
        re = jnp.dot(ucb, bre_ref[cb], preferred_element_type=F32)
        im = jnp.dot(ucb, bim_ref[cb], preferred_element_type=F32)
        for jj in range(per_cb):
            bu_ref[cb * per_cb + jj] = re[:, jj * LANES:(jj + 1) * LANES]
            bu_ref[SSM_TILES + cb * per_cb + jj] = im[:, jj * LANES:(jj + 1) * LANES]

    per_pass = SSM_TILES // n_pass
    for ps in range(n_pass):
        tiles = range(ps * per_pass, (ps + 1) * per_pass)
        ar = [ar_ref[:, j * LANES:(j + 1) * LANES] for j in tiles]
        ai = [ai_ref[:, j * LANES:(j + 1) * LANES] for j in tiles]

        def step(t, carry, tiles=tiles, ar=ar, ai=ai):
            out = []
            rows_t = pl.ds(pl.multiple_of(t * nb, nb), nb)
            for n, j in enumerate(tiles):
                hr, hi = carry[2 * n], carry[2 * n + 1]
                nr = ar[n] * hr - ai[n] * hi + bu_ref[j, rows_t, :]
                ni = ar[n] * hi + ai[n] * hr + bu_ref[SSM_TILES + j, rows_t, :]
                hs_ref[j, rows_t, :] = nr
                hs_ref[SSM_TILES + j, rows_t, :] = ni
                out += [nr, ni]
            return tuple(out)

        init = []
        for j in tiles:
            init += [st_ref[j], st_ref[SSM_TILES + j]]
        fin = lax.fori_loop(0, tc, step, tuple(init), unroll=2)
        for n, j in enumerate(tiles):
            st_ref[j] = fin[2 * n]
            st_ref[SSM_TILES + j] = fin[2 * n + 1]

    for cb in range(SSM_NCB):
        tiles = range(cb * per_cb, (cb + 1) * per_cb)
        h_re = jnp.concatenate([hs_ref[j] for j in tiles], axis=1).astype(MXU_DTYPE)
        h_im = jnp.concatenate([hs_ref[SSM_TILES + j] for j in tiles], axis=1).astype(MXU_DTYPE)
        acc = (d_ref[:, cb * LANES:(cb + 1) * LANES] * u[:, cb * LANES:(cb + 1) * LANES]
               + jnp.dot(h_re, cre_ref[cb], preferred_element_type=F32)
               + jnp.dot(h_im, cim_ref[cb], preferred_element_type=F32))
        y_ref[:, :, cb * LANES:(cb + 1) * LANES] = jnp.swapaxes(acc.reshape(tc, nb, LANES), 0, 1)

    @pl.when(ci == pl.num_programs(1) - 1)
    def _():
        for j in range(SSM_TILES):
            hr_ref[:, j * LANES:(j + 1) * LANES] = st_ref[j]
            hi_ref[:, j * LANES:(j + 1) * LANES] = st_ref[SSM_TILES + j]


def _ssm_block_weights(bbar_re, bbar_im, c_re, c_im):
    eye = jnp.eye(SSM_CBLK, dtype=F32)

    def b_blk(b):
        bt = b.reshape(SSM_NCB, SSM_CBLK, SSM_STATE, SSM_GROUP).transpose(0, 1, 3, 2)
        return (bt[:, :, :, None, :] * eye[None, :, None, :, None]).reshape(SSM_NCB, LANES, SSM_CB_COLS).astype(MXU_DTYPE)

    def c_blk(c):
        ct = c.reshape(SSM_NCB, SSM_CBLK, SSM_GROUP, SSM_STATE).transpose(0, 1, 3, 2)
        return (ct[:, :, :, None, :] * eye[None, :, None, :, None]).reshape(SSM_NCB, SSM_CB_COLS, LANES).astype(MXU_DTYPE)

    return b_blk(bbar_re), b_blk(bbar_im), c_blk(c_re), c_blk(-c_im)


def _ssm_scan(u, h0_re, h0_im, abar_re, abar_im, blocks, d_skip, tc, n_pass):
    nb_tot, t_tot, _ = u.shape
    nb = SUBLANES
    assert nb_tot % nb == 0 and t_tot % tc == 0 and tc % SUBLANES == 0
    bre, bim, cre, cim = blocks
    rows = nb * tc
    cst = lambda shape: pl.BlockSpec(shape, lambda g, c: (0,) * len(shape))
    st_spec = pl.BlockSpec((nb, SSM_COLS), lambda g, c: (g, 0))
    u_spec = pl.BlockSpec((nb, tc, SSM_WIDTH), lambda g, c: (g, c, 0))
    return pl.pallas_call(
        functools.partial(_ssm_kernel, tc=tc, n_pass=n_pass),
        grid=(nb_tot // nb, t_tot // tc),
        in_specs=[u_spec, st_spec, st_spec, cst((1, SSM_COLS)), cst((1, SSM_COLS)),
                  cst(bre.shape), cst(bim.shape), cst(cre.shape), cst(cim.shape), cst((1, SSM_WIDTH))],
        out_specs=[u_spec, st_spec, st_spec],
        out_shape=[jax.ShapeDtypeStruct(u.shape, F32),
                   jax.ShapeDtypeStruct((nb_tot, SSM_COLS), F32),
                   jax.ShapeDtypeStruct((nb_tot, SSM_COLS), F32)],
        scratch_shapes=[pltpu.VMEM((2 * SSM_TILES, rows, LANES), F32),
                        pltpu.VMEM((2 * SSM_TILES, rows, LANES), F32),
                        pltpu.VMEM((2 * SSM_TILES, nb, LANES), F32)],
        compiler_params=_cparams(("arbitrary", "arbitrary")),
        name="ssm_scan",
    )(u, h0_re, h0_im, abar_re, abar_im, bre, bim, cre, cim, d_skip)


def _rms(x, g):
    return x * lax.rsqrt(jnp.mean(x * x, axis=-1, keepdims=True) + EPS) * g


def _mxu(a, w_ref):
    return jnp.dot(a.astype(MXU_DTYPE), w_ref[...], preferred_element_type=F32)


def _tail_kernel(x_ref, attn_ref, ypre_ref, p_ref, gmix_ref, gffn_ref, gple_ref, gfin_ref,
                 wga_ref, wgb_ref, wa_ref, wglu_ref, bglu_ref, wb_ref, wout_ref,
                 wfg_ref, wfu_ref, wfd_ref, wpg_ref, wpp_ref, o_ref, *, ff_chunks, final_norm):
    x = x_ref[...]
    h = _rms(x, gmix_ref[...]).astype(MXU_DTYPE)
    branch_a = _mxu(attn_ref[...], wa_ref)
    y = jax.nn.gelu(ypre_ref[...])
    y = y * jax.nn.sigmoid(_mxu(y, wglu_ref) + bglu_ref[...])
    branch_b = _mxu(y, wb_ref)
    merged = (jax.nn.sigmoid(jnp.dot(h, wga_ref[...], preferred_element_type=F32)) * branch_a
              + jax.nn.sigmoid(jnp.dot(h, wgb_ref[...], preferred_element_type=F32)) * branch_b)
    x = x + _mxu(merged, wout_ref)
    h = _rms(x, gffn_ref[...]).astype(MXU_DTYPE)
    d_ff = wfg_ref.shape[1]
    fc = d_ff // ff_chunks
    ffn = jnp.zeros_like(x)
    for c in range(ff_chunks):
        gate = jnp.dot(h, wfg_ref[:, c * fc:(c + 1) * fc], preferred_element_type=F32)
        up = jnp.dot(h, wfu_ref[:, c * fc:(c + 1) * fc], preferred_element_type=F32)
        act = (jax.nn.silu(gate) * up).astype(MXU_DTYPE)
        ffn = ffn + jnp.dot(act, wfd_ref[c * fc:(c + 1) * fc, :], preferred_element_type=F32)
    x = x + ffn
    h = _rms(x, gple_ref[...])
    x = x + jax.nn.sigmoid(_mxu(h, wpg_ref)) * _mxu(p_ref[...], wpp_ref)
    o_ref[...] = _rms(x, gfin_ref[...]) if final_norm else x


def _tail(x, attn, ypre, p, gains, weights, tm, final_norm, ff_chunks=2):
    rows, d = x.shape
    assert rows % tm == 0
    row = lambda a: pl.BlockSpec((tm, a.shape[1]), lambda i: (i, 0))
    cst = lambda a: pl.BlockSpec(a.shape, lambda i: (0, 0), pipeline_mode=pl.Buffered(1))
    return pl.pallas_call(
        functools.partial(_tail_kernel, ff_chunks=ff_chunks, final_norm=final_norm),
        grid=(rows // tm,),
        in_specs=[row(x), row(attn), row(ypre), row(p)] + [cst(g) for g in gains] + [cst(w) for w in weights],
        out_specs=row(x),
        out_shape=jax.ShapeDtypeStruct(x.shape, F32),
        compiler_params=_cparams(("arbitrary",)),
        name="merge_ffn_ple",
    )(x, attn, ypre, p, *gains, *weights)


SCORE_CHUNK = 1024
KV_RING = 32
PAGE_UNROLL = 8


def _sample_attn_kernel(pt_ref, q_ref, qi_ref, kn_ref, vn_ref, kiwi_ref, kidx_hbm, ck_hbm, cv_hbm, o_ref,
                        kib_ref, kv_ref, sc_ref, sel_ref, lg_ref, acc_ref, tie_ref, kisem, kvsem,
                        *, group, t_new, n_pages, n_sel, n_steps):
    step = pl.program_id(0)
    past = n_pages * PAGE_SIZE
    width = past + LANES
    rows = group * t_new
    hq = N_HEADS * t_new
    chunks_per_seq = 2 * n_pages
    total_chunks = n_steps * group * chunks_per_seq
    lookahead = KV_RING - 1

    def ki_copy(s, g, page):
        src = kidx_hbm.at[pt_ref[(s * group + g) * n_pages + page]]
        dst = kib_ref.at[s % 2, g, :, pl.ds(pl.multiple_of(page * PAGE_SIZE, PAGE_SIZE), PAGE_SIZE)]
        return pltpu.make_async_copy(src, dst, kisem.at[s % 2])

    def for_ki_pages(s, fn):
        for g in range(group):
            def body(page, c, g=g):
                fn(ki_copy(s, g, page))
                return c
            lax.fori_loop(0, n_pages, body, 0)

    assert n_pages & (n_pages - 1) == 0 and KV_RING & (KV_RING - 1) == 0
    page_shift = n_pages.bit_length() - 1

    def kv_copy(gc, src_hbm):
        seq = lax.shift_right_logical(gc, page_shift + 1)
        page = gc & (n_pages - 1)
        slot = gc & (KV_RING - 1)
        return pltpu.make_async_copy(src_hbm.at[pt_ref[(seq << page_shift) + page]], kv_ref.at[slot], kvsem.at[slot])

    def issue_kv(gc):
        is_k = (gc & n_pages) == 0

        @pl.when((gc < total_chunks) & is_k)
        def _():
            kv_copy(gc, ck_hbm).start()

        @pl.when((gc < total_chunks) & jnp.logical_not(is_k))
        def _():
            kv_copy(gc, cv_hbm).start()

    @pl.when(step == 0)
    def _():
        for_ki_pages(step, lambda cp: cp.start())
        for c in range(lookahead):
            issue_kv(jnp.int32(c))

    for_ki_pages(step, lambda cp: cp.wait())

    @pl.when(step + 1 < n_steps)
    def _():
        for_ki_pages(step + 1, lambda cp: cp.start())

    lane1 = lax.broadcasted_iota(I32, (t_new, LANES), 1)
    row1 = lax.broadcasted_iota(I32, (t_new, LANES), 0)
    new_visible = lane1 <= row1

    def pad_rows(a):
        return jnp.concatenate([a, jnp.zeros((LANES - t_new, a.shape[1]), a.dtype)], axis=0)

    qi_hq, w_col = [], []
    for g in range(group):
        qi_g = qi_ref[g * t_new:(g + 1) * t_new, :].astype(MXU_DTYPE)
        qi_hq.append(jnp.concatenate([qi_g[:, h * IDX_DIM:(h + 1) * IDX_DIM] for h in range(IDX_HEADS)], axis=0))
        w_col.append([kiwi_ref[g * t_new:(g + 1) * t_new, IDX_DIM + h:IDX_DIM + h + 1] for h in range(IDX_HEADS)])

    def head_sum(s, w):
        acc = jnp.zeros((t_new, s.shape[1]), F32)
        for h in range(IDX_HEADS):
            acc = acc + jnp.maximum(s[h * t_new:(h + 1) * t_new, :], 0.0) * w[h]
        return acc

    def score_chunk(c, carry):
        off = pl.multiple_of(c * SCORE_CHUNK, SCORE_CHUNK)
        for g in range(group):
            kc = kib_ref[step % 2, g, :, pl.ds(off, SCORE_CHUNK)].astype(MXU_DTYPE)
            s = jnp.dot(qi_hq[g], kc, preferred_element_type=F32)
            sc_ref[g * t_new:(g + 1) * t_new, pl.ds(off, SCORE_CHUNK)] = head_sum(s, w_col[g])
        return carry

    lax.fori_loop(0, past // SCORE_CHUNK, score_chunk, 0)
    for g in range(group):
        ki_new = pad_rows(kiwi_ref[g * t_new:(g + 1) * t_new, :IDX_DIM]).astype(MXU_DTYPE)
        s_new = jnp.where(new_visible, head_sum(_nt_dot(qi_hq[g], ki_new), w_col[g]), NEG)
        sc_ref[g * t_new:(g + 1) * t_new, past:width] = jnp.where(lane1 < t_new, s_new, -jnp.inf)

    n_tiles = width // LANES
    n_acc = 4

    def count(pred):
        accs = [jnp.zeros((rows, LANES), I32)] * n_acc
        for j in range(n_tiles):
            accs[j % n_acc] = accs[j % n_acc] + pred(sc_ref[:, j * LANES:(j + 1) * LANES], j * LANES).astype(I32)
        return jnp.sum((accs[0] + accs[1]) + (accs[2] + accs[3]), axis=1, keepdims=True)

    thr = _kth_largest(lambda cand, key: count(lambda kk, off: kk >= cand), n_sel, (rows, 1))
    cnt_ge = count(lambda kk, off: kk >= thr)
    need = n_sel - count(lambda kk, off: kk > thr)

    idx_bits = (width - 1).bit_length()
    lane_r = lax.broadcasted_iota(I32, (rows, LANES), 1)
    tie_ref[...] = jnp.full((rows, 1), (1 << idx_bits) - 1, I32)

    @pl.when(jnp.max(cnt_ge) > n_sel)
    def _():
        def idx_step(i, x):
            cand = x | jnp.left_shift(jnp.int32(1), idx_bits - 1 - i)
            c = count(lambda kk, off: (kk == thr) & ((lane_r + off) < cand))
            return jnp.where(c < need, cand, x)
        tie_ref[...] = lax.fori_loop(0, idx_bits, idx_step, jnp.zeros((rows, 1), I32))

    tie_x = tie_ref[...]
    new_vis_rows = jnp.concatenate([new_visible] * group, axis=0)
    for j in range(n_tiles):
        kk = sc_ref[:, j * LANES:(j + 1) * LANES]
        taken = jnp.where(kk > thr, 1, jnp.where(kk == thr, jnp.where((lane_r + j * LANES) <= tie_x, 1, 0), 0))
        if j == n_tiles - 1:
            taken = jnp.where(new_vis_rows, taken, 0)
        sel_ref[:, j * LANES:(j + 1) * LANES] = taken.astype(F32)

    head_of_row = lax.broadcasted_iota(I32, (hq, ATTN_WIDTH), 0) // t_new
    head_of_lane = lax.broadcasted_iota(I32, (hq, ATTN_WIDTH), 1) // HEAD_DIM
    for g in range(group):
        r0 = g * t_new
        chunk0 = (step * group + g) * chunks_per_seq
        q_g = q_ref[r0:r0 + t_new, :]
        q_bd = jnp.where(head_of_row == head_of_lane, jnp.concatenate([q_g] * N_HEADS, axis=0), 0.0).astype(MXU_DTYPE)

        def sel_hq(lo, r0=r0):
            s = sel_ref[r0:r0 + t_new, pl.ds(lo, LANES)]
            return jnp.concatenate([s] * N_HEADS, axis=0) > 0.0

        def k_pages(it, m, chunk0=chunk0, q_bd=q_bd, sel_hq=sel_hq):
            gc0 = chunk0 + it * PAGE_UNROLL
            for un in range(PAGE_UNROLL):
                kv_copy(gc0 + un, ck_hbm).wait()
            for un in range(PAGE_UNROLL):
                kp = kv_ref[(gc0 + un) & (KV_RING - 1)].astype(MXU_DTYPE)
                off = pl.multiple_of((it * PAGE_UNROLL + un) * PAGE_SIZE, PAGE_SIZE)
                lg = jnp.where(sel_hq(off), jnp.dot(q_bd, kp, preferred_element_type=F32), NEG)
                lg_ref[:, pl.ds(off, LANES)] = lg
                m = jnp.maximum(m, lg)
            for un in range(PAGE_UNROLL):
                issue_kv(gc0 + un + lookahead)
            return m

        m = lax.fori_loop(0, n_pages // PAGE_UNROLL, k_pages, jnp.full((hq, LANES), -jnp.inf, F32))
        k_new = pad_rows(kn_ref[r0:r0 + t_new, :]).astype(MXU_DTYPE)
        lg_new = jnp.where(sel_hq(past), _nt_dot(q_bd, k_new), NEG)
        lg_ref[:, past:width] = lg_new
        m = jnp.max(jnp.maximum(m, lg_new), axis=1, keepdims=True)

        acc_ref[...] = jnp.zeros_like(acc_ref)

        def v_pages(it, den, chunk0=chunk0, m=m):
            gc0 = chunk0 + n_pages + it * PAGE_UNROLL
            for un in range(PAGE_UNROLL):
                kv_copy(gc0 + un, ck_hbm).wait()
            acc = acc_ref[...]
            for un in range(PAGE_UNROLL):
                vp = kv_ref[(gc0 + un) & (KV_RING - 1)].astype(MXU_DTYPE)
                off = pl.multiple_of((it * PAGE_UNROLL + un) * PAGE_SIZE, PAGE_SIZE)
                e = jnp.exp(lg_ref[:, pl.ds(off, LANES)] - m)
                acc = acc + _nt_dot(e.astype(MXU_DTYPE), vp)
                den = den + e
            acc_ref[...] = acc
            for un in range(PAGE_UNROLL):
                issue_kv(gc0 + un + lookahead)
            return den

        den = lax.fori_loop(0, n_pages // PAGE_UNROLL, v_pages, jnp.zeros((hq, LANES), F32))
        e_new = jnp.exp(lg_ref[:, past:width] - m)
        v_new = pad_rows(vn_ref[r0:r0 + t_new, :]).astype(MXU_DTYPE)
        acc = acc_ref[...] + jnp.dot(e_new.astype(MXU_DTYPE), v_new, preferred_element_type=F32)
        o = acc / jnp.sum(den + e_new, axis=1, keepdims=True)
        o = jnp.where(head_of_row == head_of_lane, o, 0.0)
        out = o[0:t_new, :]
        for h in range(1, N_HEADS):
            out = out + o[h * t_new:(h + 1) * t_new, :]
        o_ref[r0:r0 + t_new, :] = out


def _sample_attention(q, qi, k_new, v_new, kiwi, kidx_pages, k_pages, v_pages, page_table, t_new, group):
    rows_tot = q.shape[0]
    n_seq, n_pages = page_table.shape
    assert n_seq % group == 0 and (n_pages * PAGE_SIZE) % SCORE_CHUNK == 0 and t_new <= SUBLANES
    assert n_pages % PAGE_UNROLL == 0
    n_steps = n_seq // group
    rows = group * t_new
    past = n_pages * PAGE_SIZE
    width = past + LANES
    n_sel = min(TOPK_MAX, (past + t_new) // 4)
    row = lambda w: pl.BlockSpec((rows, w), lambda i, pt: (i, 0))
    hbm = pl.BlockSpec(memory_space=pl.ANY)
    grid_spec = pltpu.PrefetchScalarGridSpec(
        num_scalar_prefetch=1,
        grid=(n_steps,),
        in_specs=[row(ATTN_WIDTH), row(ATTN_WIDTH), row(ATTN_WIDTH), row(ATTN_WIDTH), row(LANES), hbm, hbm, hbm],
        out_specs=row(ATTN_WIDTH),
        scratch_shapes=[pltpu.VMEM((2, group, IDX_DIM, past), F32),
                        pltpu.VMEM((KV_RING, ATTN_WIDTH, PAGE_SIZE), F32),
                        pltpu.VMEM((rows, width), F32),
                        pltpu.VMEM((rows, width), F32),
                        pltpu.VMEM((N_HEADS * t_new, width), F32),
                        pltpu.VMEM((N_HEADS * t_new, ATTN_WIDTH), F32),
                        pltpu.VMEM((rows, 1), I32),
                        pltpu.SemaphoreType.DMA((2,)),
                        pltpu.SemaphoreType.DMA((KV_RING,))])
    return pl.pallas_call(
        functools.partial(_sample_attn_kernel, group=group, t_new=t_new, n_pages=n_pages, n_sel=n_sel,
                          n_steps=n_steps),
        grid_spec=grid_spec,
        out_shape=jax.ShapeDtypeStruct((rows_tot, ATTN_WIDTH), F32),
        compiler_params=_cparams(("arbitrary",)),
        name="sample_attention",
    )(page_table.reshape(-1), q, qi, k_new, v_new, kiwi, kidx_pages, k_pages, v_pages)


IN_SIZES = (ATTN_WIDTH, ATTN_WIDTH, ATTN_WIDTH, IDX_HEADS * IDX_DIM, IDX_DIM, IDX_HEADS, SSM_WIDTH)
PROMPT_ROW_TILE = 512
SAMPLE_GROUP = 4
SSM_TIME_CHUNK = 128
SSM_PASSES = 2


def _split_w_in(w):
    d = w.shape[0]
    offs = np.cumsum((0,) + IN_SIZES)
    o_ki, o_u, o_ga = int(offs[4]), int(offs[6]), int(offs[7])
    kiwi = jnp.pad(w[:, o_ki:o_u], ((0, 0), (0, LANES - (o_u - o_ki))))
    w_main = jnp.concatenate([w[:, :o_ki], kiwi, w[:, o_u:o_ga]], axis=1).astype(MXU_DTYPE)
    return w_main, w[:, o_ga:o_ga + d].astype(MXU_DTYPE), w[:, o_ga + d:o_ga + 2 * d].astype(MXU_DTYPE)


def kernel(x_prompt, x_sample, p_prompt, p_sample, cache_k, cache_v, cache_kidx, state_ssm_re, state_ssm_im, page_table, norm_mix, w_in, w_branch_a, w_branch_b, w_out, ssm_a_re, ssm_a_im, ssm_log_dt, ssm_b_re, ssm_b_im, ssm_c_re, ssm_c_im, ssm_d, w_glu, b_glu, norm_ffn, w_ff_gate, w_ff_up, w_ff_down, norm_ple, w_ple_gate, w_ple_proj, norm_final):
    b, s, d = x_prompt.shape
    db, t, _ = x_sample.shape
    depth = w_in.shape[0]
    n_pages = page_table.shape[1]
    past = n_pages * PAGE_SIZE
    n_phys = cache_k.shape[1]
    rows_p, rows_s = b * s, db * t
    tm_p = min(PROMPT_ROW_TILE, s)
    tm_s = min(PROMPT_ROW_TILE, rows_s)
    tables_p = _rope_tables(s, 0, s)
    tables_s = _rope_tables(tm_s, past, t)
    bf = lambda a: a.astype(MXU_DTYPE)
    row = lambda a: a.reshape(1, -1)

    xp = x_prompt.reshape(rows_p, d)
    xs = x_sample.reshape(rows_s, d)
    st_p, st_s = [], []
    for i in range(depth):
        w_main, w_ga, w_gb = _split_w_in(w_in[i])
        abr, abi, bbr, bbi = _ssm_discretize(ssm_a_re[i], ssm_a_im[i], ssm_log_dt[i], ssm_b_re[i], ssm_b_im[i])
        blocks = _ssm_block_weights(bbr, bbi, ssm_c_re[i], ssm_c_im[i])
        gains = [row(norm_mix[i]), row(norm_ffn[i]), row(norm_ple[i]), row(norm_final)]
        weights = [w_ga, w_gb, bf(w_branch_a[i]), bf(w_glu[i]), row(b_glu[i]), bf(w_branch_b[i]), bf(w_out[i]),
                   bf(w_ff_gate[i]), bf(w_ff_up[i]), bf(w_ff_down[i]), bf(w_ple_gate[i]), bf(w_ple_proj[i])]
        final = i == depth - 1

        (q, kbf, kt, _, vt, vtbf, qi, _, kiwit, kidup, u) = _inproj(
            xp.reshape(b, s, d), gains[0], w_main, tables_p, tm_p)
        attn = _prompt_attention(q, qi, kiwit, kbf, kidup, vtbf, b, s)
        zeros = jnp.zeros((b, SSM_COLS), F32)
        ypre, hr, hi = _ssm_scan(u.reshape(b, s, SSM_WIDTH), zeros, zeros, row(abr), row(abi), blocks,
                                 row(ssm_d[i]), min(SSM_TIME_CHUNK, s), SSM_PASSES)
        xp = _tail(xp, attn, ypre.reshape(rows_p, SSM_WIDTH), p_prompt[i].reshape(rows_p, -1), gains, weights,
                   tm_p, final)
        heads = lambda a: a.reshape(b, N_HEADS, HEAD_DIM, s).transpose(0, 3, 1, 2)
        st_p.append((heads(kt), heads(vt), kiwit[:, :IDX_DIM, :].transpose(0, 2, 1),
                     hr.reshape(b, SSM_GROUPS, SSM_STATE), hi.reshape(b, SSM_GROUPS, SSM_STATE)))

        (q, kbf, kt, vbf, vt, _, qi, kiwi, _, _, u) = _inproj(
            xs.reshape(1, rows_s, d), gains[0], w_main, tables_s, tm_s)
        f32 = lambda a: a.astype(F32)
        attn = _sample_attention(
            f32(q), f32(qi), f32(kbf), f32(vbf), kiwi,
            cache_kidx[i].transpose(0, 2, 1),
            cache_k[i].transpose(0, 2, 3, 1).reshape(n_phys, ATTN_WIDTH, PAGE_SIZE),
            cache_v[i].transpose(0, 2, 3, 1).reshape(n_phys, ATTN_WIDTH, PAGE_SIZE),
            page_table, t, SAMPLE_GROUP)
        ypre, hr, hi = _ssm_scan(u.reshape(db, t, SSM_WIDTH), state_ssm_re[i].reshape(db, SSM_COLS),
                                 state_ssm_im[i].reshape(db, SSM_COLS), row(abr), row(abi), blocks,
                                 row(ssm_d[i]), t, SSM_PASSES)
        xs = _tail(xs, attn, ypre.reshape(rows_s, SSM_WIDTH), p_sample[i].reshape(rows_s, -1), gains, weights,
                   tm_s, final)
        heads = lambda a: a.reshape(N_HEADS, HEAD_DIM, db, t).transpose(2, 3, 0, 1)
        st_s.append((heads(kt), heads(vt), kiwi[:, :IDX_DIM].reshape(db, t, IDX_DIM),
                     hr.reshape(db, SSM_GROUPS, SSM_STATE), hi.reshape(db, SSM_GROUPS, SSM_STATE)))

    stack = lambda sts, j: jnp.stack([st[j] for st in sts])
    return (xp.reshape(b, s, d), xs.reshape(db, t, d),
            *[stack(st_p, j) for j in range(5)], *[stack(st_s, j) for j in range(5)])
```

```python
import functools
import math

import numpy as np
import jax
import jax.numpy as jnp
from jax import lax
from jax.experimental import pallas as pl
from jax.experimental.pallas import tpu as pltpu

N_HEADS = 8
HEAD_DIM = 64
ATTN_WIDTH = N_HEADS * HEAD_DIM
IDX_HEADS = 8
IDX_DIM = 64
TOPK_MAX = 256
ROPE_THETA = 500000.0
ROPE_ROT = HEAD_DIM // 4
ROPE_HALF = ROPE_ROT // 2
SSM_GROUP = 16
SSM_GROUPS = 32
SSM_WIDTH = SSM_GROUP * SSM_GROUPS
SSM_STATE = 64
SSM_COLS = SSM_GROUPS * SSM_STATE
PAGE_SIZE = 128
Q_BLOCK = 128
EPS = 1e-6
NEG = -1e30

LANES = 128
SUBLANES = 8
VMEM_LIMIT = 56 * 1024 * 1024

MXU_DTYPE = jnp.bfloat16
F32 = jnp.float32
I32 = jnp.int32
INT_MIN = -(2 ** 31)


def _key_to_f32(key):
    return pltpu.bitcast(key ^ ((key >> 31) & 0x7FFFFFFF), F32)


def _kth_largest(count_ge, n_sel, shape):
    zero = jnp.zeros(shape, I32)
    base = jnp.where(count_ge(_key_to_f32(zero), zero) >= n_sel, 0, INT_MIN).astype(I32)

    def bit_step(i, base):
        cand = base | jnp.left_shift(jnp.int32(1), 30 - i)
        return jnp.where(count_ge(_key_to_f32(cand), cand) >= n_sel, cand, base)

    return _key_to_f32(lax.fori_loop(0, 31, bit_step, base))


def _cparams(sem):
    return pltpu.CompilerParams(dimension_semantics=sem, vmem_limit_bytes=VMEM_LIMIT)


def _nt_dot(a, b):
    return lax.dot_general(a, b, (((1,), (1,)), ((), ())), preferred_element_type=F32)


def _rope_table_kernel(invf_ref, c_ref, s1_ref, s2_ref, *, pos0, period, rows):
    i = pl.program_id(0)
    r = lax.broadcasted_iota(I32, (rows, LANES), 0) + i * rows
    pos = (pos0 + r % period).astype(F32)
    ang = pos * invf_ref[...]
    lane = lax.broadcasted_iota(I32, (rows, LANES), 1) % HEAD_DIM
    c = jnp.cos(ang)
    s = jnp.sin(ang)
    c_ref[...] = jnp.where(lane < ROPE_ROT, c, 1.0)
    s1_ref[...] = jnp.where(lane < ROPE_HALF, -s, 0.0)
    s2_ref[...] = jnp.where((lane >= ROPE_HALF) & (lane < ROPE_ROT), s, 0.0)


def _rope_tables(n_rows, pos0, period):
    inv_freq = ROPE_THETA ** (-jnp.arange(0, ROPE_ROT, 2, dtype=F32) / ROPE_ROT)
    lane = np.arange(LANES) % HEAD_DIM
    invf = jnp.where(lane < ROPE_ROT, inv_freq[lane % ROPE_HALF], 0.0).astype(F32)[None, :]
    rows = min(n_rows, 512)
    assert n_rows % rows == 0
    spec = pl.BlockSpec((rows, LANES), lambda i: (i, 0))
    return pl.pallas_call(
        functools.partial(_rope_table_kernel, pos0=pos0, period=period, rows=rows),
        grid=(n_rows // rows,),
        in_specs=[pl.BlockSpec((1, LANES), lambda i: (0, 0))],
        out_specs=[spec, spec, spec],
        out_shape=[jax.ShapeDtypeStruct((n_rows, LANES), F32)] * 3,
        compiler_params=_cparams(("arbitrary",)),
        name="rope_tables",
    )(invf)


W_Q, W_K, W_V, W_QI, W_KIWI, W_U = 0, 512, 1024, 1536, 2048, 2176
W_MAIN = 2688


def _inproj_kernel(x_ref, g_ref, w_ref, c_ref, s1_ref, s2_ref,
                   q_ref, kbf_ref, kt_ref, vbf_ref, vt_ref, vtbf_ref, qi_ref,
                   kiwi_ref, kiwit_ref, kidup_ref, u_ref):
    x = x_ref[...]
    ms = jnp.mean(x * x, axis=-1, keepdims=True)
    h = (x * lax.rsqrt(ms + EPS) * g_ref[...]).astype(MXU_DTYPE)
    cos_t, sin_lo, sin_hi = c_ref[...], s1_ref[...], s2_ref[...]

    def proj(lo, hi):
        return jnp.dot(h, w_ref[:, lo:hi], preferred_element_type=F32)

    def rope_tile(t):
        return t * cos_t + pltpu.roll(t, LANES - ROPE_HALF, 1) * sin_lo + pltpu.roll(t, ROPE_HALF, 1) * sin_hi

    def rope(z):
        return jnp.concatenate([rope_tile(z[:, j * LANES:(j + 1) * LANES]) for j in range(z.shape[1] // LANES)], axis=1)

    q_ref[...] = (rope(proj(W_Q, W_K)) * (HEAD_DIM ** -0.5)).astype(q_ref.dtype)
    k = rope(proj(W_K, W_V))
    kbf_ref[...] = k.astype(kbf_ref.dtype)
    kt_ref[...] = k.T
    v = proj(W_V, W_QI)
    vbf_ref[...] = v.astype(vbf_ref.dtype)
    vt = v.T
    vt_ref[...] = vt
    vtbf_ref[...] = vt.astype(vtbf_ref.dtype)
    qi_ref[...] = (rope(proj(W_QI, W_KIWI)) * (IDX_DIM ** -0.5)).astype(qi_ref.dtype)
    z = proj(W_KIWI, W_U)
    lane = lax.broadcasted_iota(I32, z.shape, 1)
    kiwi = jnp.where(lane < IDX_DIM, rope_tile(z), z * (IDX_HEADS ** -0.5))
    kiwi_ref[...] = kiwi
    kiwit_ref[...] = kiwi.T
    kidup_ref[...] = jnp.where(lane < IDX_DIM, kiwi, pltpu.roll(kiwi, IDX_DIM, 1)).astype(kidup_ref.dtype)
    u_ref[...] = proj(W_U, W_MAIN)


def _inproj(x, gain, w_main, tables, tm):
    bk, sk, d = x.shape
    nt = sk // tm
    assert sk % tm == 0
    cos_t, sin_lo, sin_hi = tables
    n_tab = cos_t.shape[0] // tm
    rows = bk * sk
    xf = x.reshape(rows, d)
    row = lambda w: pl.BlockSpec((tm, w), lambda i: (i, 0))
    tr = lambda w: pl.BlockSpec((None, w, tm), lambda i: (i // nt, 0, i % nt))
    tab = pl.BlockSpec((tm, LANES), lambda i: (i % n_tab, 0))
    outs = [
        (row(ATTN_WIDTH), (rows, ATTN_WIDTH), MXU_DTYPE),
        (row(ATTN_WIDTH), (rows, ATTN_WIDTH), MXU_DTYPE),
        (tr(ATTN_WIDTH), (bk, ATTN_WIDTH, sk), F32),
        (row(ATTN_WIDTH), (rows, ATTN_WIDTH), MXU_DTYPE),
        (tr(ATTN_WIDTH), (bk, ATTN_WIDTH, sk), F32),
        (tr(ATTN_WIDTH), (bk, ATTN_WIDTH, sk), MXU_DTYPE),
        (row(ATTN_WIDTH), (rows, ATTN_WIDTH), MXU_DTYPE),
        (row(LANES), (rows, LANES), F32),
        (tr(LANES), (bk, LANES, sk), F32),
        (row(LANES), (rows, LANES), MXU_DTYPE),
        (row(SSM_WIDTH), (rows, SSM_WIDTH), F32),
    ]
    return pl.pallas_call(
        _inproj_kernel,
        grid=(rows // tm,),
        in_specs=[row(d),
                  pl.BlockSpec((1, d), lambda i: (0, 0)),
                  pl.BlockSpec((d, W_MAIN), lambda i: (0, 0)),
                  tab, tab, tab],
        out_specs=[o[0] for o in outs],
        out_shape=[jax.ShapeDtypeStruct(o[1], o[2]) for o in outs],
        compiler_params=_cparams(("arbitrary",)),
        name="inproj",
    )(xf, gain, w_main, cos_t, sin_lo, sin_hi)


KEY_CHUNK = 512


def _head_masked_pair(tile, lane_lo):
    zero = jnp.zeros_like(tile)
    return jnp.concatenate([jnp.where(lane_lo, tile, zero), jnp.where(lane_lo, zero, tile)], axis=0)


def _prompt_attn_kernel(q_ref, qi_ref, kw_ref, k_ref, kidup_ref, vt_ref, o_ref,
                        sc_ref, lg_ref, ot_ref, rhs_ref, wi_ref, tie_ref, *, seq, n_sel):
    qb = pl.program_id(1)
    lane_lo = lax.broadcasted_iota(I32, (Q_BLOCK, LANES), 1) < HEAD_DIM
    lane_q = lax.broadcasted_iota(I32, (1, Q_BLOCK), 1)
    row_iota = lax.broadcasted_iota(I32, (KEY_CHUNK, Q_BLOCK), 0)
    n_pairs = N_HEADS // 2
    groups = KEY_CHUNK // SUBLANES

    def n_chunks(blk):
        return ((blk + 1) * Q_BLOCK + KEY_CHUNK - 1) // KEY_CHUNK

    nch = n_chunks(qb)
    n_virtual = seq - nch * KEY_CHUNK
    qpos = qb * Q_BLOCK + lane_q

    def chunk_rows(c):
        return pl.ds(pl.multiple_of(c * KEY_CHUNK, KEY_CHUNK), KEY_CHUNK)

    def group_reduce(op, a):
        return op(a.reshape(groups, SUBLANES, a.shape[1]), axis=0)

    for p in range(n_pairs):
        rhs_ref[p] = _head_masked_pair(qi_ref[:, p * LANES:(p + 1) * LANES], lane_lo)
        rhs_ref[n_pairs + p] = _head_masked_pair(q_ref[:, p * LANES:(p + 1) * LANES], lane_lo)
    wi_ref[...] = kw_ref[IDX_DIM:IDX_DIM + IDX_HEADS, :]

    def produce_chunk(c, carry):
        rows = chunk_rows(c)
        kd = kidup_ref[rows, :]
        acc = jnp.zeros((KEY_CHUNK, Q_BLOCK), F32)
        for p in range(n_pairs):
            s = _nt_dot(kd, rhs_ref[p])
            acc = acc + jnp.maximum(s[:, :Q_BLOCK], 0.0) * wi_ref[2 * p:2 * p + 1, :]
            acc = acc + jnp.maximum(s[:, Q_BLOCK:], 0.0) * wi_ref[2 * p + 1:2 * p + 2, :]
        visible = (row_iota + c * KEY_CHUNK) <= qpos
        sc_ref[rows, :] = jnp.where(visible, acc, NEG)
        for p in range(n_pairs):
            lg = _nt_dot(k_ref[rows, p * LANES:(p + 1) * LANES], rhs_ref[n_pairs + p])
            lg_ref[p, rows, :] = lg.astype(lg_ref.dtype)
        return carry

    lax.fori_loop(0, nch, produce_chunk, 0)

    def count(pred):
        def body(c, acc):
            m = pred(sc_ref[chunk_rows(c), :], c * KEY_CHUNK).astype(I32)
            return acc + group_reduce(jnp.sum, m)
        acc = lax.fori_loop(0, nch, body, jnp.zeros((SUBLANES, Q_BLOCK), I32))
        return jnp.sum(acc, axis=0, keepdims=True)

    def count_ge(cand, key):
        return count(lambda sc, off: sc >= cand) + jnp.where(cand <= NEG, n_virtual, 0)

    thr = _kth_largest(count_ge, n_sel, (1, Q_BLOCK))
    cnt_ge = count_ge(thr, None)
    cnt_gt = count(lambda sc, off: sc > thr) + jnp.where(thr < NEG, n_virtual, 0)
    need = n_sel - cnt_gt

    idx_bits = max(1, (seq - 1).bit_length())
    tie_ref[...] = jnp.full((1, Q_BLOCK), (1 << idx_bits) - 1, I32)

    @pl.when(jnp.max(cnt_ge) > n_sel)
    def _():
        def idx_step(i, x):
            cand = x | jnp.left_shift(jnp.int32(1), idx_bits - 1 - i)
            c = count(lambda kk, off: (kk == thr) & ((row_iota + off) < cand))
            return jnp.where(c < need, cand, x)
        tie_ref[...] = lax.fori_loop(0, idx_bits, idx_step, jnp.zeros((1, Q_BLOCK), I32))

    tie_x = tie_ref[...]

    def mask_chunk(c, ms):
        ms = list(ms)
        for t in range(KEY_CHUNK // Q_BLOCK):
            rows = pl.ds(pl.multiple_of(c * KEY_CHUNK + t * Q_BLOCK, Q_BLOCK), Q_BLOCK)
            sc = sc_ref[rows, :]
            idx = row_iota[:Q_BLOCK] + (c * KEY_CHUNK + t * Q_BLOCK)
            taken = jnp.where(sc > thr, 1, jnp.where(sc == thr, jnp.where(idx <= tie_x, 1, 0), 0))
            sel = jnp.where(idx <= qpos, taken, 0) > 0
            for p in range(n_pairs):
                lo = jnp.where(sel, lg_ref[p, rows, :Q_BLOCK], NEG)
                hi = jnp.where(sel, lg_ref[p, rows, Q_BLOCK:], NEG)
                lg_ref[p, rows, :Q_BLOCK] = lo
                lg_ref[p, rows, Q_BLOCK:] = hi
                tile_max = lambda a: jnp.max(a.reshape(Q_BLOCK // SUBLANES, SUBLANES, Q_BLOCK), axis=0)
                ms[2 * p] = jnp.maximum(ms[2 * p], tile_max(lo))
                ms[2 * p + 1] = jnp.maximum(ms[2 * p + 1], tile_max(hi))
        return tuple(ms)

    ms = lax.fori_loop(0, nch, mask_chunk, (jnp.full((SUBLANES, Q_BLOCK), -jnp.inf, F32),) * N_HEADS)
    ms = [jnp.max(m, axis=0, keepdims=True) for m in ms]

    ot_ref[...] = jnp.zeros_like(ot_ref)

    def pv_chunk(c, dens):
        rows = chunk_rows(c)
        out = []
        for h in range(N_HEADS):
            half = slice((h % 2) * Q_BLOCK, (h % 2 + 1) * Q_BLOCK)
            e = jnp.exp(lg_ref[h // 2, rows, half] - ms[h])
            out.append(dens[h] + group_reduce(jnp.sum, e))
            v = vt_ref[h * HEAD_DIM:(h + 1) * HEAD_DIM, rows]
            ot_ref[h * HEAD_DIM:(h + 1) * HEAD_DIM, :] += jnp.dot(v, e.astype(MXU_DTYPE), preferred_element_type=F32)
        return tuple(out)

    dens = lax.fori_loop(0, nch, pv_chunk, (jnp.zeros((SUBLANES, Q_BLOCK), F32),) * N_HEADS)
    for h in range(N_HEADS):
        den = jnp.sum(dens[h], axis=0, keepdims=True)
        ot_ref[h * HEAD_DIM:(h + 1) * HEAD_DIM, :] = ot_ref[h * HEAD_DIM:(h + 1) * HEAD_DIM, :] / den
    o_ref[...] = ot_ref[...].T.astype(o_ref.dtype)


def _prompt_attention(q, qi, kiwit, k, kidup, vt, batch, seq):
    nqb = seq // Q_BLOCK
    assert seq % KEY_CHUNK == 0
    n_sel = min(TOPK_MAX, seq // 4)
    qspec = pl.BlockSpec((Q_BLOCK, ATTN_WIDTH), lambda b, j: (b * nqb + j, 0))
    return pl.pallas_call(
        functools.partial(_prompt_attn_kernel, seq=seq, n_sel=n_sel),
        grid=(batch, nqb),
        in_specs=[qspec, qspec,
                  pl.BlockSpec((None, LANES, Q_BLOCK), lambda b, j: (b, 0, j)),
                  pl.BlockSpec((seq, ATTN_WIDTH), lambda b, j: (b, 0)),
                  pl.BlockSpec((seq, LANES), lambda b, j: (b, 0)),
                  pl.BlockSpec((None, ATTN_WIDTH, seq), lambda b, j: (b, 0, 0))],
        out_specs=qspec,
        out_shape=jax.ShapeDtypeStruct((batch * seq, ATTN_WIDTH), MXU_DTYPE),
        scratch_shapes=[pltpu.VMEM((seq, Q_BLOCK), F32),
                        pltpu.VMEM((N_HEADS // 2, seq, 2 * Q_BLOCK), F32),
                        pltpu.VMEM((ATTN_WIDTH, Q_BLOCK), F32),
                        pltpu.VMEM((N_HEADS, 2 * Q_BLOCK, LANES), MXU_DTYPE),
                        pltpu.VMEM((IDX_HEADS, Q_BLOCK), F32),
                        pltpu.VMEM((1, Q_BLOCK), I32)],
        compiler_params=_cparams(("arbitrary", "arbitrary")),
        name="prompt_attention",
    )(q, qi, kiwit, k, kidup, vt)


def _ssm_disc_kernel(are_ref, aim_ref, ldt_ref, bre_ref, bim_ref, abr_ref, abi_ref, bbr_ref, bbi_ref):
    are, aim = are_ref[...], aim_ref[...]
    dt = jnp.exp(ldt_ref[...])
    mag = jnp.exp(are * dt)
    abr = mag * jnp.cos(aim * dt)
    abi = mag * jnp.sin(aim * dt)
    abr_ref[...] = abr
    abi_ref[...] = abi
    nr, ni = abr - 1.0, abi
    den = are * are + aim * aim
    fr = (nr * are + ni * aim) / den
    fi = (ni * are - nr * aim) / den
    bre, bim = bre_ref[...], bim_ref[...]
    bbr_ref[...] = fr * bre - fi * bim
    bbi_ref[...] = fr * bim + fi * bre


def _ssm_discretize(a_re, a_im, log_dt, b_re, b_im):
    col = lambda a: a.reshape(SSM_COLS, 1)
    ldt = jnp.broadcast_to(log_dt[:, None], (SSM_GROUPS, SSM_STATE))
    flat = lambda b: b.reshape(SSM_COLS, SSM_GROUP)
    shp1 = jax.ShapeDtypeStruct((SSM_COLS, 1), F32)
    shpb = jax.ShapeDtypeStruct((SSM_COLS, SSM_GROUP), F32)
    return pl.pallas_call(_ssm_disc_kernel, out_shape=[shp1, shp1, shpb, shpb], name="ssm_discretize")(
        col(a_re), col(a_im), col(ldt), flat(b_re), flat(b_im))


SSM_CBLK = LANES // SSM_GROUP
SSM_NCB = SSM_WIDTH // LANES
SSM_TILES = SSM_COLS // LANES
SSM_CB_COLS = SSM_CBLK * SSM_STATE


def _ssm_kernel(u_ref, h0r_ref, h0i_ref, ar_ref, ai_ref, bre_ref, bim_ref, cre_ref, cim_ref, d_ref,
                y_ref, hr_ref, hi_ref, bu_ref, hs_ref, st_ref, *, tc, n_pass):
    ci = pl.program_id(1)
    nb = u_ref.shape[0]
    rows = nb * tc

    @pl.when(ci == 0)
    def _():
        for j in range(SSM_TILES):
            st_ref[j] = h0r_ref[:, j * LANES:(j + 1) * LANES]
            st_ref[SSM_TILES + j] = h0i_ref[:, j * LANES:(j + 1) * LANES]

    u = jnp.swapaxes(u_ref[...], 0, 1).reshape(rows, SSM_WIDTH)
    ub = u.astype(MXU_DTYPE)
    per_cb = SSM_CB_COLS // LANES
    for cb in range(SSM_NCB):
        ucb = ub[:, cb * LANES:(cb + 1) * LANES]
        re = jnp.dot(ucb, bre_ref[cb], preferred_element_type=F32)
        im = jnp.dot(ucb, bim_ref[cb], preferred_element_type=F32)
        for jj in range(per_cb):
            bu_ref[cb * per_cb + jj] = re[:, jj * LANES:(jj + 1) * LANES]
            bu_ref[SSM_TILES + cb * per_cb + jj] = im[:, jj * LANES:(jj + 1) * LANES]

    per_pass = SSM_TILES // n_pass
    for ps in range(n_pass):
        tiles = range(ps * per_pass, (ps + 1) * per_pass)
        ar = [ar_ref[:, j * LANES:(j + 1) * LANES] for j in tiles]
        ai = [ai_ref[:, j * LANES:(j + 1) * LANES] for j in tiles]

        def step(t, carry, tiles=tiles, ar=ar, ai=ai):
            out = []
            rows_t = pl.ds(pl.multiple_of(t * nb, nb), nb)
            for n, j in enumerate(tiles):
                hr, hi = carry[2 * n], carry[2 * n + 1]
                nr = ar[n] * hr - ai[n] * hi + bu_ref[j, rows_t, :]
                ni = ar[n] * hi + ai[n] * hr + bu_ref[SSM_TILES + j, rows_t, :]
                hs_ref[j, rows_t, :] = nr
                hs_ref[SSM_TILES + j, rows_t, :] = ni
                out += [nr, ni]
            return tuple(out)

        init = []
        for j in tiles:
            init += [st_ref[j], st_ref[SSM_TILES + j]]
        fin = lax.fori_loop(0, tc, step, tuple(init), unroll=2)
        for n, j in enumerate(tiles):
            st_ref[j] = fin[2 * n]
            st_ref[SSM_TILES + j] = fin[2 * n + 1]

    for cb in range(SSM_NCB):
        tiles = range(cb * per_cb, (cb + 1) * per_cb)
        h_re = jnp.concatenate([hs_ref[j] for j in tiles], axis=1).astype(MXU_DTYPE)
        h_im = jnp.concatenate([hs_ref[SSM_TILES + j] for j in tiles], axis=1).astype(MXU_DTYPE)
        acc = (d_ref[:, cb * LANES:(cb + 1) * LANES] * u[:, cb * LANES:(cb + 1) * LANES]
               + jnp.dot(h_re, cre_ref[cb], preferred_element_type=F32)
               + jnp.dot(h_im, cim_ref[cb], preferred_element_type=F32))
        y_ref[:, :, cb * LANES:(cb + 1) * LANES] = jnp.swapaxes(acc.reshape(tc, nb, LANES), 0, 1)

    @pl.when(ci == pl.num_programs(1) - 1)
    def _():
        for j in range(SSM_TILES):
            hr_ref[:, j * LANES:(j + 1) * LANES] = st_ref[j]
            hi_ref[:, j * LANES:(j + 1) * LANES] = st_ref[SSM_TILES + j]


def _ssm_block_weights(bbar_re, bbar_im, c_re, c_im):
    eye = jnp.eye(SSM_CBLK, dtype=F32)

    def b_blk(b):
        bt = b.reshape(SSM_NCB, SSM_CBLK, SSM_STATE, SSM_GROUP).transpose(0, 1, 3, 2)
        return (bt[:, :, :, None, :] * eye[None, :, None, :, None]).reshape(SSM_NCB, LANES, SSM_CB_COLS).astype(MXU_DTYPE)

    def c_blk(c):
        ct = c.reshape(SSM_NCB, SSM_CBLK, SSM_GROUP, SSM_STATE).transpose(0, 1, 3, 2)
        return (ct[:, :, :, None, :] * eye[None, :, None, :, None]).reshape(SSM_NCB, SSM_CB_COLS, LANES).astype(MXU_DTYPE)

    return b_blk(bbar_re), b_blk(bbar_im), c_blk(c_re), c_blk(-c_im)


def _ssm_scan(u, h0_re, h0_im, abar_re, abar_im, blocks, d_skip, tc, n_pass):
    nb_tot, t_tot, _ = u.shape
    nb = SUBLANES
    assert nb_tot % nb == 0 and t_tot % tc == 0 and tc % SUBLANES == 0
    bre, bim, cre, cim = blocks
    rows = nb * tc
    cst = lambda shape: pl.BlockSpec(shape, lambda g, c: (0,) * len(shape))
    st_spec = pl.BlockSpec((nb, SSM_COLS), lambda g, c: (g, 0))
    u_spec = pl.BlockSpec((nb, tc, SSM_WIDTH), lambda g, c: (g, c, 0))
    return pl.pallas_call(
        functools.partial(_ssm_kernel, tc=tc, n_pass=n_pass),
        grid=(nb_tot // nb, t_tot // tc),
        in_specs=[u_spec, st_spec, st_spec, cst((1, SSM_COLS)), cst((1, SSM_COLS)),
                  cst(bre.shape), cst(bim.shape), cst(cre.shape), cst(cim.shape), cst((1, SSM_WIDTH))],
        out_specs=[u_spec, st_spec, st_spec],
        out_shape=[jax.ShapeDtypeStruct(u.shape, F32),
                   jax.ShapeDtypeStruct((nb_tot, SSM_COLS), F32),
                   jax.ShapeDtypeStruct((nb_tot, SSM_COLS), F32)],
        scratch_shapes=[pltpu.VMEM((2 * SSM_TILES, rows, LANES), F32),
                        pltpu.VMEM((2 * SSM_TILES, rows, LANES), F32),
                        pltpu.VMEM((2 * SSM_TILES, nb, LANES), F32)],
        compiler_params=_cparams(("arbitrary", "arbitrary")),
        name="ssm_scan",
    )(u, h0_re, h0_im, abar_re, abar_im, bre, bim, cre, cim, d_skip)


def _rms(x, g):
    return x * lax.rsqrt(jnp.mean(x * x, axis=-1, keepdims=True) + EPS) * g


def _mxu(a, w_ref):
    return jnp.dot(a.astype(MXU_DTYPE), w_ref[...], preferred_element_type=F32)


def _tail_kernel(x_ref, attn_ref, ypre_ref, p_ref, gmix_ref, gffn_ref, gple_ref, gfin_ref,
                 wga_ref, wgb_ref, wa_ref, wglu_ref, bglu_ref, wb_ref, wout_ref,
                 wfg_ref, wfu_ref, wfd_ref, wpg_ref, wpp_ref, o_ref, *, ff_chunks, final_norm):
    x = x_ref[...]
    h = _rms(x, gmix_ref[...]).astype(MXU_DTYPE)
    branch_a = _mxu(attn_ref[...], wa_ref)
    y = jax.nn.gelu(ypre_ref[...])
    y = y * jax.nn.sigmoid(_mxu(y, wglu_ref) + bglu_ref[...])
    branch_b = _mxu(y, wb_ref)
    merged = (jax.nn.sigmoid(jnp.dot(h, wga_ref[...], preferred_element_type=F32)) * branch_a
              + jax.nn.sigmoid(jnp.dot(h, wgb_ref[...], preferred_element_type=F32)) * branch_b)
    x = x + _mxu(merged, wout_ref)
    h = _rms(x, gffn_ref[...]).astype(MXU_DTYPE)
    d_ff = wfg_ref.shape[1]
    fc = d_ff // ff_chunks
    ffn = jnp.zeros_like(x)
    for c in range(ff_chunks):
        gate = jnp.dot(h, wfg_ref[:, c * fc:(c + 1) * fc], preferred_element_type=F32)
        up = jnp.dot(h, wfu_ref[:, c * fc:(c + 1) * fc], preferred_element_type=F32)
        act = (jax.nn.silu(gate) * up).astype(MXU_DTYPE)
        ffn = ffn + jnp.dot(act, wfd_ref[c * fc:(c + 1) * fc, :], preferred_element_type=F32)
    x = x + ffn
    h = _rms(x, gple_ref[...])
    x = x + jax.nn.sigmoid(_mxu(h, wpg_ref)) * _mxu(p_ref[...], wpp_ref)
    o_ref[...] = _rms(x, gfin_ref[...]) if final_norm else x


def _tail(x, attn, ypre, p, gains, weights, tm, final_norm, ff_chunks=2):
    rows, d = x.shape
    assert rows % tm == 0
    row = lambda a: pl.BlockSpec((tm, a.shape[1]), lambda i: (i, 0))
    cst = lambda a: pl.BlockSpec(a.shape, lambda i: (0, 0), pipeline_mode=pl.Buffered(1))
    return pl.pallas_call(
        functools.partial(_tail_kernel, ff_chunks=ff_chunks, final_norm=final_norm),
        grid=(rows // tm,),
        in_specs=[row(x), row(attn), row(ypre), row(p)] + [cst(g) for g in gains] + [cst(w) for w in weights],
        out_specs=row(x),
        out_shape=jax.ShapeDtypeStruct(x.shape, F32),
        compiler_params=_cparams(("arbitrary",)),
        name="merge_ffn_ple",
    )(x, attn, ypre, p, *gains, *weights)


SCORE_CHUNK = 1024
KV_RING = 32
PAGE_UNROLL = 8


def _sample_attn_kernel(pt_ref, q_ref, qi_ref, kn_ref, vn_ref, kiwi_ref, kidx_hbm, ck_hbm, cv_hbm, o_ref,
                        kib_ref, kv_ref, sc_ref, sel_ref, lg_ref, acc_ref, tie_ref, kisem, kvsem,
                        *, group, t_new, n_pages, n_sel, n_steps):
    step = pl.program_id(0)
    past = n_pages * PAGE_SIZE
    width = past + LANES
    rows = group * t_new
    hq = N_HEADS * t_new
    chunks_per_seq = 2 * n_pages
    total_chunks = n_steps * group * chunks_per_seq
    lookahead = KV_RING - 1

    def ki_copy(s, g, page):
        src = kidx_hbm.at[pt_ref[(s * group + g) * n_pages + page]]
        dst = kib_ref.at[s % 2, g, :, pl.ds(pl.multiple_of(page * PAGE_SIZE, PAGE_SIZE), PAGE_SIZE)]
        return pltpu.make_async_copy(src, dst, kisem.at[s % 2])

    def for_ki_pages(s, fn):
        for g in range(group):
            def body(page, c, g=g):
                fn(ki_copy(s, g, page))
                return c
            lax.fori_loop(0, n_pages, body, 0)

    def kv_copy(gc, src_hbm):
        seq = gc // chunks_per_seq
        page = gc % n_pages
        return pltpu.make_async_copy(src_hbm.at[pt_ref[seq * n_pages + page]], kv_ref.at[gc % KV_RING],
                                     kvsem.at[gc % KV_RING])

    def issue_kv(gc):
        is_k = (gc % chunks_per_seq) < n_pages

        @pl.when((gc < total_chunks) & is_k)
        def _():
            kv_copy(gc, ck_hbm).start()

        @pl.when((gc < total_chunks) & jnp.logical_not(is_k))
        def _():
            kv_copy(gc, cv_hbm).start()

    @pl.when(step == 0)
    def _():
        for_ki_pages(step, lambda cp: cp.start())
        for c in range(lookahead):
            issue_kv(jnp.int32(c))

    for_ki_pages(step, lambda cp: cp.wait())

    @pl.when(step + 1 < n_steps)
    def _():
        for_ki_pages(step + 1, lambda cp: cp.start())

    lane1 = lax.broadcasted_iota(I32, (t_new, LANES), 1)
    row1 = lax.broadcasted_iota(I32, (t_new, LANES), 0)
    new_visible = lane1 <= row1

    def pad_rows(a):
        return jnp.concatenate([a, jnp.zeros((LANES - t_new, a.shape[1]), a.dtype)], axis=0)

    qi_hq, w_col = [], []
    for g in range(group):
        qi_g = qi_ref[g * t_new:(g + 1) * t_new, :].astype(MXU_DTYPE)
        qi_hq.append(jnp.concatenate([qi_g[:, h * IDX_DIM:(h + 1) * IDX_DIM] for h in range(IDX_HEADS)], axis=0))
        w_col.append([kiwi_ref[g * t_new:(g + 1) * t_new, IDX_DIM + h:IDX_DIM + h + 1] for h in range(IDX_HEADS)])

    def head_sum(s, w):
        acc = jnp.zeros((t_new, s.shape[1]), F32)
        for h in range(IDX_HEADS):
            acc = acc + jnp.maximum(s[h * t_new:(h + 1) * t_new, :], 0.0) * w[h]
        return acc

    def score_chunk(c, carry):
        off = pl.multiple_of(c * SCORE_CHUNK, SCORE_CHUNK)
        for g in range(group):
            kc = kib_ref[step % 2, g, :, pl.ds(off, SCORE_CHUNK)].astype(MXU_DTYPE)
            s = jnp.dot(qi_hq[g], kc, preferred_element_type=F32)
            sc_ref[g * t_new:(g + 1) * t_new, pl.ds(off, SCORE_CHUNK)] = head_sum(s, w_col[g])
        return carry

    lax.fori_loop(0, past // SCORE_CHUNK, score_chunk, 0)
    for g in range(group):
        ki_new = pad_rows(kiwi_ref[g * t_new:(g + 1) * t_new, :IDX_DIM]).astype(MXU_DTYPE)
        s_new = jnp.where(new_visible, head_sum(_nt_dot(qi_hq[g], ki_new), w_col[g]), NEG)
        sc_ref[g * t_new:(g + 1) * t_new, past:width] = jnp.where(lane1 < t_new, s_new, -jnp.inf)

    n_tiles = width // LANES
    n_acc = 4

    def count(pred):
        accs = [jnp.zeros((rows, LANES), I32)] * n_acc
        for j in range(n_tiles):
            accs[j % n_acc] = accs[j % n_acc] + pred(sc_ref[:, j * LANES:(j + 1) * LANES], j * LANES).astype(I32)
        return jnp.sum((accs[0] + accs[1]) + (accs[2] + accs[3]), axis=1, keepdims=True)

    thr = _kth_largest(lambda cand, key: count(lambda kk, off: kk >= cand), n_sel, (rows, 1))
    cnt_ge = count(lambda kk, off: kk >= thr)
    need = n_sel - count(lambda kk, off: kk > thr)

    idx_bits = (width - 1).bit_length()
    lane_r = lax.broadcasted_iota(I32, (rows, LANES), 1)
    tie_ref[...] = jnp.full((rows, 1), (1 << idx_bits) - 1, I32)

    @pl.when(jnp.max(cnt_ge) > n_sel)
    def _():
        def idx_step(i, x):
            cand = x | jnp.left_shift(jnp.int32(1), idx_bits - 1 - i)
            c = count(lambda kk, off: (kk == thr) & ((lane_r + off) < cand))
            return jnp.where(c < need, cand, x)
        tie_ref[...] = lax.fori_loop(0, idx_bits, idx_step, jnp.zeros((rows, 1), I32))

    tie_x = tie_ref[...]
    new_vis_rows = jnp.concatenate([new_visible] * group, axis=0)
    for j in range(n_tiles):
        kk = sc_ref[:, j * LANES:(j + 1) * LANES]
        taken = jnp.where(kk > thr, 1, jnp.where(kk == thr, jnp.where((lane_r + j * LANES) <= tie_x, 1, 0), 0))
        if j == n_tiles - 1:
            taken = jnp.where(new_vis_rows, taken, 0)
        sel_ref[:, j * LANES:(j + 1) * LANES] = taken.astype(F32)

    head_of_row = lax.broadcasted_iota(I32, (hq, ATTN_WIDTH), 0) // t_new
    head_of_lane = lax.broadcasted_iota(I32, (hq, ATTN_WIDTH), 1) // HEAD_DIM
    for g in range(group):
        r0 = g * t_new
        chunk0 = (step * group + g) * chunks_per_seq
        q_g = q_ref[r0:r0 + t_new, :]
        q_bd = jnp.where(head_of_row == head_of_lane, jnp.concatenate([q_g] * N_HEADS, axis=0), 0.0).astype(MXU_DTYPE)

        def sel_hq(lo, r0=r0):
            s = sel_ref[r0:r0 + t_new, pl.ds(lo, LANES)]
            return jnp.concatenate([s] * N_HEADS, axis=0) > 0.0

        def k_pages(it, m, chunk0=chunk0, q_bd=q_bd, sel_hq=sel_hq):
            gc0 = chunk0 + it * PAGE_UNROLL
            for un in range(PAGE_UNROLL):
                kv_copy(gc0 + un, ck_hbm).wait()
            for un in range(PAGE_UNROLL):
                kp = kv_ref[(gc0 + un) % KV_RING].astype(MXU_DTYPE)
                off = pl.multiple_of((it * PAGE_UNROLL + un) * PAGE_SIZE, PAGE_SIZE)
                lg = jnp.where(sel_hq(off), jnp.dot(q_bd, kp, preferred_element_type=F32), NEG)
                lg_ref[:, pl.ds(off, LANES)] = lg
                m = jnp.maximum(m, lg)
            for un in range(PAGE_UNROLL):
                issue_kv(gc0 + un + lookahead)
            return m

        m = lax.fori_loop(0, n_pages // PAGE_UNROLL, k_pages, jnp.full((hq, LANES), -jnp.inf, F32))
        k_new = pad_rows(kn_ref[r0:r0 + t_new, :]).astype(MXU_DTYPE)
        lg_new = jnp.where(sel_hq(past), _nt_dot(q_bd, k_new), NEG)
        lg_ref[:, past:width] = lg_new
        m = jnp.max(jnp.maximum(m, lg_new), axis=1, keepdims=True)

        acc_ref[...] = jnp.zeros_like(acc_ref)

        def v_pages(it, den, chunk0=chunk0, m=m):
            gc0 = chunk0 + n_pages + it * PAGE_UNROLL
            for un in range(PAGE_UNROLL):
                kv_copy(gc0 + un, ck_hbm).wait()
            acc = acc_ref[...]
            for un in range(PAGE_UNROLL):
                vp = kv_ref[(gc0 + un) % KV_RING].astype(MXU_DTYPE)
                off = pl.multiple_of((it * PAGE_UNROLL + un) * PAGE_SIZE, PAGE_SIZE)
                e = jnp.exp(lg_ref[:, pl.ds(off, LANES)] - m)
                acc = acc + _nt_dot(e.astype(MXU_DTYPE), vp)
                den = den + e
            acc_ref[...] = acc
            for un in range(PAGE_UNROLL):
                issue_kv(gc0 + un + lookahead)
            return den

        den = lax.fori_loop(0, n_pages // PAGE_UNROLL, v_pages, jnp.zeros((hq, LANES), F32))
        e_new = jnp.exp(lg_ref[:, past:width] - m)
        v_new = pad_rows(vn_ref[r0:r0 + t_new, :]).astype(MXU_DTYPE)
        acc = acc_ref[...] + jnp.dot(e_new.astype(MXU_DTYPE), v_new, preferred_element_type=F32)
        o = acc / jnp.sum(den + e_new, axis=1, keepdims=True)
        o = jnp.where(head_of_row == head_of_lane, o, 0.0)
        out = o[0:t_new, :]
        for h in range(1, N_HEADS):
            out = out + o[h * t_new:(h + 1) * t_new, :]
        o_ref[r0:r0 + t_new, :] = out


def _sample_attention(q, qi, k_new, v_new, kiwi, kidx_pages, k_pages, v_pages, page_table, t_new, group):
    rows_tot = q.shape[0]
    n_seq, n_pages = page_table.shape
    assert n_seq % group == 0 and (n_pages * PAGE_SIZE) % SCORE_CHUNK == 0 and t_new <= SUBLANES
    assert n_pages % PAGE_UNROLL == 0
    n_steps = n_seq // group
    rows = group * t_new
    past = n_pages * PAGE_SIZE
    width = past + LANES
    n_sel = min(TOPK_MAX, (past + t_new) // 4)
    row = lambda w: pl.BlockSpec((rows, w), lambda i, pt: (i, 0))
    hbm = pl.BlockSpec(memory_space=pl.ANY)
    grid_spec = pltpu.PrefetchScalarGridSpec(
        num_scalar_prefetch=1,
        grid=(n_steps,),
        in_specs=[row(ATTN_WIDTH), row(ATTN_WIDTH), row(ATTN_WIDTH), row(ATTN_WIDTH), row(LANES), hbm, hbm, hbm],
        out_specs=row(ATTN_WIDTH),
        scratch_shapes=[pltpu.VMEM((2, group, IDX_DIM, past), F32),
                        pltpu.VMEM((KV_RING, ATTN_WIDTH, PAGE_SIZE), F32),
                        pltpu.VMEM((rows, width), F32),
                        pltpu.VMEM((rows, width), F32),
                        pltpu.VMEM((N_HEADS * t_new, width), F32),
                        pltpu.VMEM((N_HEADS * t_new, ATTN_WIDTH), F32),
                        pltpu.VMEM((rows, 1), I32),
                        pltpu.SemaphoreType.DMA((2,)),
                        pltpu.SemaphoreType.DMA((KV_RING,))])
    return pl.pallas_call(
        functools.partial(_sample_attn_kernel, group=group, t_new=t_new, n_pages=n_pages, n_sel=n_sel,
                          n_steps=n_steps),
        grid_spec=grid_spec,
        out_shape=jax.ShapeDtypeStruct((rows_tot, ATTN_WIDTH), F32),
        compiler_params=_cparams(("arbitrary",)),
        name="sample_attention",
    )(page_table.reshape(-1), q, qi, k_new, v_new, kiwi, kidx_pages, k_pages, v_pages)


IN_SIZES = (ATTN_WIDTH, ATTN_WIDTH, ATTN_WIDTH, IDX_HEADS * IDX_DIM, IDX_DIM, IDX_HEADS, SSM_WIDTH)
PROMPT_ROW_TILE = 512
SAMPLE_GROUP = 4
SSM_TIME_CHUNK = 128
SSM_PASSES = 2


def _split_w_in(w):
    d = w.shape[0]
    offs = np.cumsum((0,) + IN_SIZES)
    o_ki, o_u, o_ga = int(offs[4]), int(offs[6]), int(offs[7])
    kiwi = jnp.pad(w[:, o_ki:o_u], ((0, 0), (0, LANES - (o_u - o_ki))))
    w_main = jnp.concatenate([w[:, :o_ki], kiwi, w[:, o_u:o_ga]], axis=1).astype(MXU_DTYPE)
    return w_main, w[:, o_ga:o_ga + d].astype(MXU_DTYPE), w[:, o_ga + d:o_ga + 2 * d].astype(MXU_DTYPE)


def kernel(x_prompt, x_sample, p_prompt, p_sample, cache_k, cache_v, cache_kidx, state_ssm_re, state_ssm_im, page_table, norm_mix, w_in, w_branch_a, w_branch_b, w_out, ssm_a_re, ssm_a_im, ssm_log_dt, ssm_b_re, ssm_b_im, ssm_c_re, ssm_c_im, ssm_d, w_glu, b_glu, norm_ffn, w_ff_gate, w_ff_up, w_ff_down, norm_ple, w_ple_gate, w_ple_proj, norm_final):
    b, s, d = x_prompt.shape
    db, t, _ = x_sample.shape
    depth = w_in.shape[0]
    n_pages = page_table.shape[1]
    past = n_pages * PAGE_SIZE
    n_phys = cache_k.shape[1]
    rows_p, rows_s = b * s, db * t
    tm_p = min(PROMPT_ROW_TILE, s)
    tm_s = min(PROMPT_ROW_TILE, rows_s)
    tables_p = _rope_tables(s, 0, s)
    tables_s = _rope_tables(tm_s, past, t)
    bf = lambda a: a.astype(MXU_DTYPE)
    row = lambda a: a.reshape(1, -1)

    xp = x_prompt.reshape(rows_p, d)
    xs = x_sample.reshape(rows_s, d)
    st_p, st_s = [], []
    for i in range(depth):
        w_main, w_ga, w_gb = _split_w_in(w_in[i])
        abr, abi, bbr, bbi = _ssm_discretize(ssm_a_re[i], ssm_a_im[i], ssm_log_dt[i], ssm_b_re[i], ssm_b_im[i])
        blocks = _ssm_block_weights(bbr, bbi, ssm_c_re[i], ssm_c_im[i])
        gains = [row(norm_mix[i]), row(norm_ffn[i]), row(norm_ple[i]), row(norm_final)]
        weights = [w_ga, w_gb, bf(w_branch_a[i]), bf(w_glu[i]), row(b_glu[i]), bf(w_branch_b[i]), bf(w_out[i]),
                   bf(w_ff_gate[i]), bf(w_ff_up[i]), bf(w_ff_down[i]), bf(w_ple_gate[i]), bf(w_ple_proj[i])]
        final = i == depth - 1

        (q, kbf, kt, _, vt, vtbf, qi, _, kiwit, kidup, u) = _inproj(
            xp.reshape(b, s, d), gains[0], w_main, tables_p, tm_p)
        attn = _prompt_attention(q, qi, kiwit, kbf, kidup, vtbf, b, s)
        zeros = jnp.zeros((b, SSM_COLS), F32)
        ypre, hr, hi = _ssm_scan(u.reshape(b, s, SSM_WIDTH), zeros, zeros, row(abr), row(abi), blocks,
                                 row(ssm_d[i]), min(SSM_TIME_CHUNK, s), SSM_PASSES)
        xp = _tail(xp, attn, ypre.reshape(rows_p, SSM_WIDTH), p_prompt[i].reshape(rows_p, -1), gains, weights,
                   tm_p, final)
        heads = lambda a: a.reshape(b, N_HEADS, HEAD_DIM, s).transpose(0, 3, 1, 2)
        st_p.append((heads(kt), heads(vt), kiwit[:, :IDX_DIM, :].transpose(0, 2, 1),
                     hr.reshape(b, SSM_GROUPS, SSM_STATE), hi.reshape(b, SSM_GROUPS, SSM_STATE)))

        (q, kbf, kt, vbf, vt, _, qi, kiwi, _, _, u) = _inproj(
            xs.reshape(1, rows_s, d), gains[0], w_main, tables_s, tm_s)
        f32 = lambda a: a.astype(F32)
        attn = _sample_attention(
            f32(q), f32(qi), f32(kbf), f32(vbf), kiwi,
            cache_kidx[i].transpose(0, 2, 1),
            cache_k[i].transpose(0, 2, 3, 1).reshape(n_phys, ATTN_WIDTH, PAGE_SIZE),
            cache_v[i].transpose(0, 2, 3, 1).reshape(n_phys, ATTN_WIDTH, PAGE_SIZE),
            page_table, t, SAMPLE_GROUP)
        ypre, hr, hi = _ssm_scan(u.reshape(db, t, SSM_WIDTH), state_ssm_re[i].reshape(db, SSM_COLS),
                                 state_ssm_im[i].reshape(db, SSM_COLS), row(abr), row(abi), blocks,
                                 row(ssm_d[i]), t, SSM_PASSES)
        xs = _tail(xs, attn, ypre.reshape(rows_s, SSM_WIDTH), p_sample[i].reshape(rows_s, -1), gains, weights,
                   tm_s, final)
        heads = lambda a: a.reshape(N_HEADS, HEAD_DIM, db, t).transpose(2, 3, 0, 1)
        st_s.append((heads(kt), heads(vt), kiwi[:, :IDX_DIM].reshape(db, t, IDX_DIM),
                     hr.reshape(db, SSM_GROUPS, SSM_STATE), hi.reshape(db, SSM_GROUPS, SSM_STATE)))

    stack = lambda sts, j: jnp.stack([st[j] for st in sts])
    return (xp.reshape(b, s, d), xs.reshape(db, t, d),
            *[stack(st_p, j) for j in range(5)], *[stack(st_s, j) for j in range(5)])
```
